```python
import jax, jax.numpy as jnp
from jax import lax
import numpy as np

D_MODEL = 1024
BATCH = 4
SEQ = 4096
DEPTH = 4
DEC_BATCH = 128
DEC_SEQ = 1
PAST_LEN = 8192
PAGE_SIZE = 128

F32 = jnp.float32
HEAD_DIM = 64
ROT_DIM = HEAD_DIM // 4
ROPE_THETA = 500000.0
MLA_HEADS = 8
MLA_NOPE = 64
MLA_ROPE = 32
MLA_V = 64
Q_LORA = 256
KV_LORA = 128
MLA_CACHE_W = KV_LORA + MLA_ROPE
MLA_SCALE = (MLA_NOPE + MLA_ROPE) ** -0.5
DSA_HEADS = 8
DSA_KV_HEADS = 2
DSA_GROUP = DSA_HEADS // DSA_KV_HEADS
IDX_HEADS = 8
IDX_DIM = 64
IDX_ROT = IDX_DIM // 4
IDX_W_SCALE = IDX_HEADS ** -0.5 * IDX_DIM ** -0.5
TOPK_MAX = 256
ATTN_SCALE = HEAD_DIM ** -0.5
DIL_HEADS = 16
DIL_PATTERNS = ((128, 1), (512, 4), (2048, 16))
WIN_MAX = 2048
D_FF = -(-8 * D_MODEL // (3 * 256)) * 256
AB_SPLITS = (Q_LORA, KV_LORA, MLA_ROPE,
             DSA_HEADS * HEAD_DIM, DSA_KV_HEADS * HEAD_DIM, DSA_KV_HEADS * HEAD_DIM,
             IDX_HEADS * IDX_DIM, IDX_DIM, IDX_HEADS)
AB_IN = sum(AB_SPLITS)
AB_OUT = MLA_HEADS * MLA_V + DSA_HEADS * HEAD_DIM
C_IN = 3 * DIL_HEADS * HEAD_DIM
C_OUT = DIL_HEADS * HEAD_DIM
N_AB_LAYERS = (DEPTH + 1) // 2
N_C_LAYERS = DEPTH // 2
DEEPNORM_ALPHA = (2 * DEPTH) ** 0.25
DEEPNORM_BETA = (8 * DEPTH) ** -0.25
QB = 128

kernel_name = "hybrid_mla_dsa_dilated_decoder_step"


def layer_norm(x, g, b, eps=1e-5):
    xf = x.astype(F32)
    mu = jnp.mean(xf, -1, keepdims=True)
    var = jnp.mean(jnp.square(xf - mu), -1, keepdims=True)
    return ((xf - mu) * lax.rsqrt(var + eps) * g + b).astype(x.dtype)


def rms_norm(x, g, eps=1e-6):
    xf = x.astype(F32)
    return (xf * lax.rsqrt(jnp.mean(xf * xf, -1, keepdims=True) + eps) * g).astype(x.dtype)


def rotary(x, pos, rot_dim):
    half = rot_dim // 2
    inv_freq = ROPE_THETA ** (-(jnp.arange(half, dtype=F32) * 2.0 / rot_dim))
    ang = pos.astype(F32)[:, None] * inv_freq[None, :]
    cos, sin = jnp.cos(ang)[:, None, :], jnp.sin(ang)[:, None, :]
    xr = x[..., :rot_dim].astype(F32)
    x1, x2 = xr[..., :half], xr[..., half:]
    rot = jnp.concatenate([x1 * cos - x2 * sin, x2 * cos + x1 * sin], -1).astype(x.dtype)
    return jnp.concatenate([rot, x[..., rot_dim:]], -1)


def split_cols(z, sizes):
    outs, start = [], 0
    for s in sizes:
        outs.append(z[..., start:start + s])
        start += s
    return outs


def batch_gather(a, idx):
    return jax.vmap(lambda ab, ib: ab[ib])(a, idx)


def map_query_blocks(fn, qs, pos):
    n_blk = pos.shape[0] // QB
    def split(a):
        return jnp.swapaxes(a.reshape(a.shape[0], n_blk, QB, *a.shape[2:]), 0, 1)
    out = lax.map(lambda args: fn(*args[0], args[1]),
                  (tuple(split(a) for a in qs), pos.reshape(n_blk, QB)))
    out = jnp.swapaxes(out, 0, 1)
    return out.reshape(out.shape[0], n_blk * QB, *out.shape[3:])


def ab_project(h, pos, w_in, q_g, w_q_b, kv_g, w_kv_b, ik_g, ik_b):
    n_b, t = h.shape[:2]
    q_a, kv_a, k_r, q_s, k_s, v_s, q_i, k_i, w_i = split_cols(h @ w_in, AB_SPLITS)
    q = jnp.einsum('btr,rhe->bthe', rms_norm(q_a, q_g), w_q_b)
    q_rope = rotary(q[..., MLA_NOPE:], pos, MLA_ROPE)
    q_lat = jnp.einsum('bthn,chn->bthc', q[..., :MLA_NOPE], w_kv_b[..., :MLA_NOPE])
    mla_q = jnp.concatenate([q_lat, q_rope], -1)
    k_rope = rotary(k_r[:, :, None, :], pos, MLA_ROPE)[:, :, 0]
    mla_rows = jnp.concatenate([rms_norm(kv_a, kv_g), k_rope], -1)
    dsa_q = rotary(q_s.reshape(n_b, t, DSA_HEADS, HEAD_DIM), pos, ROT_DIM)
    dsa_k = rotary(k_s.reshape(n_b, t, DSA_KV_HEADS, HEAD_DIM), pos, ROT_DIM)
    dsa_v = v_s.reshape(n_b, t, DSA_KV_HEADS, HEAD_DIM)
    idx_q = rotary(q_i.reshape(n_b, t, IDX_HEADS, IDX_DIM), pos, IDX_ROT)
    idx_k = rotary(layer_norm(k_i, ik_g, ik_b)[:, :, None, :], pos, IDX_ROT)[:, :, 0]
    idx_w = w_i * IDX_W_SCALE
    return mla_q, mla_rows, dsa_q, dsa_k, dsa_v, idx_q, idx_k, idx_w


def ab_merge(o_lat, o_dsa, w_kv_b, w_o):
    o_mla = jnp.einsum('bthc,chv->bthv', o_lat, w_kv_b[..., MLA_NOPE:])
    n_b, t = o_mla.shape[:2]
    o = jnp.concatenate([o_mla.reshape(n_b, t, -1), o_dsa.reshape(n_b, t, -1)], -1)
    return o @ w_o


def mla_attend(q_cat, rows, q_pos, k_pos):
    s = jnp.einsum('bqhc,bkc->bhqk', q_cat, rows, preferred_element_type=F32) * MLA_SCALE
    s = jnp.where(k_pos[None, :] <= q_pos[:, None], s, -jnp.inf)
    p = jax.nn.softmax(s, axis=-1).astype(rows.dtype)
    return jnp.einsum('bhqk,bkc->bqhc', p, rows[..., :KV_LORA])


def indexer_scores(qi, wi, ki, q_pos, k_pos):
    s = jnp.einsum('bqhd,bkd->bqhk', qi, ki, preferred_element_type=F32)
    score = jnp.einsum('bqhk,bqh->bqk', jax.nn.relu(s), wi.astype(F32))
    return jnp.where(k_pos[None, :] <= q_pos[:, None], score, -jnp.inf)


def sparse_attend(q, k_sel, v_sel, valid):
    n_b, t = q.shape[:2]
    qg = q.reshape(n_b, t, DSA_KV_HEADS, DSA_GROUP, HEAD_DIM)
    s = jnp.einsum('bqngd,bqknd->bqngk', qg, k_sel, preferred_element_type=F32) * ATTN_SCALE
    s = jnp.where(valid[:, :, None, None, :], s, -jnp.inf)
    p = jax.nn.softmax(s, axis=-1).astype(v_sel.dtype)
    return jnp.einsum('bqngk,bqknd->bqngd', p, v_sel).reshape(n_b, t, DSA_HEADS, HEAD_DIM)


def dsa_prompt(q, qi, wi, k, v, ki, pos):
    n_sel = min(TOPK_MAX, k.shape[1] // 4)
    def block(qb, qib, wib, pb):
        _, sel = lax.top_k(indexer_scores(qib, wib, ki, pb, pos), n_sel)
        return sparse_attend(qb, batch_gather(k, sel), batch_gather(v, sel), sel <= pb[None, :, None])
    return map_query_blocks(block, (q, qi, wi), pos)


def gather_paged_rows(pool, layer, page_table, new_rows, sel):
    past = page_table.shape[1] * PAGE_SIZE
    p = jnp.minimum(sel, past - 1)
    phys = jax.vmap(lambda pt, pg: pt[pg])(page_table, p // PAGE_SIZE)
    from_past = pool[layer, phys, p % PAGE_SIZE]
    from_new = batch_gather(new_rows, jnp.clip(sel - past, 0, new_rows.shape[1] - 1))
    is_new = (sel >= past).reshape(sel.shape + (1,) * (from_past.ndim - sel.ndim))
    return jnp.where(is_new, from_new, from_past)


def dsa_sample(q, qi, wi, k_new, v_new, ki_new, pool_k, pool_v, pool_ki, layer, page_table, pos):
    n_b = q.shape[0]
    past = page_table.shape[1] * PAGE_SIZE
    ki_all = jnp.concatenate([pool_ki[layer, page_table].reshape(n_b, past, IDX_DIM), ki_new], 1)
    n_keys = ki_all.shape[1]
    n_sel = min(TOPK_MAX, n_keys // 4)
    _, sel = lax.top_k(indexer_scores(qi, wi, ki_all, pos, jnp.arange(n_keys, dtype=jnp.int32)), n_sel)
    k_sel = gather_paged_rows(pool_k, layer, page_table, k_new, sel)
    v_sel = gather_paged_rows(pool_v, layer, page_table, v_new, sel)
    return sparse_attend(q, k_sel, v_sel, sel <= pos[None, :, None])


def c_project(h, pos, w_in):
    n_b, t = h.shape[:2]
    z = (h @ w_in).reshape(n_b, t, 3, DIL_HEADS, HEAD_DIM)
    return rotary(z[:, :, 0], pos, ROT_DIM), rotary(z[:, :, 1], pos, ROT_DIM), z[:, :, 2]


def dilated_attend(q, k_all, v_all, q_idx):
    outs, lses = [], []
    for window, dil in DIL_PATTERNS:
        idx = q_idx[:, None] - dil * jnp.arange(window // dil + 1, dtype=jnp.int32)[None, :]
        valid = idx >= 0
        idx = jnp.maximum(idx, 0)
        k_g, v_g = k_all[:, idx], v_all[:, idx]
        s = jnp.einsum('bqhd,bqmhd->bqhm', q, k_g, preferred_element_type=F32) * ATTN_SCALE
        s = jnp.where(valid[None, :, None, :], s, -jnp.inf)
        m = jnp.max(s, -1, keepdims=True)
        lse = m + jnp.log(jnp.sum(jnp.exp(s - m), -1, keepdims=True))
        p = jnp.exp(s - lse).astype(v_all.dtype)
        outs.append(jnp.einsum('bqhm,bqmhd->bqhd', p, v_g))
        lses.append(lse[..., 0])
    wts = jax.nn.softmax(jnp.stack(lses), axis=0)
    o = jnp.einsum('gbqh,gbqhd->bqhd', wts, jnp.stack(outs).astype(F32))
    return o.astype(q.dtype)


def swiglu(h, w_gate, w_up, w_down):
    return (jax.nn.silu(h @ w_gate) * (h @ w_up)) @ w_down


def deepnorm(x, f, g, b):
    return layer_norm(DEEPNORM_ALPHA * x + f, g, b)


def setup_inputs(seed: int = 0) -> dict:
    key = jax.random.key(seed)
    ks = iter(jax.random.split(key, 40))
    n_pages = PAST_LEN // PAGE_SIZE
    n_used = DEC_BATCH * n_pages
    pool_pages = n_used + n_used // 4
    win_buf = min(WIN_MAX, PAST_LEN)

    def nrm(shape, scale=1.0):
        return jax.random.normal(next(ks), shape, F32) * scale

    x_prompt = nrm((BATCH, SEQ, D_MODEL))
    x_sample = nrm((DEC_BATCH, DEC_SEQ, D_MODEL))
    cache_mla = nrm((N_AB_LAYERS, pool_pages, PAGE_SIZE, MLA_CACHE_W))
    cache_dsa_k = nrm((N_AB_LAYERS, pool_pages, PAGE_SIZE, DSA_KV_HEADS, HEAD_DIM))
    cache_dsa_v = nrm((N_AB_LAYERS, pool_pages, PAGE_SIZE, DSA_KV_HEADS, HEAD_DIM))
    cache_idx_k = nrm((N_AB_LAYERS, pool_pages, PAGE_SIZE, IDX_DIM))
    state_win_k = nrm((N_C_LAYERS, DEC_BATCH, win_buf, DIL_HEADS, HEAD_DIM))
    state_win_v = nrm((N_C_LAYERS, DEC_BATCH, win_buf, DIL_HEADS, HEAD_DIM))
    page_table = jax.random.permutation(next(ks), pool_pages)[:n_used].reshape(DEC_BATCH, n_pages).astype(jnp.int32)
    w_in_ab = nrm((N_AB_LAYERS, D_MODEL, AB_IN), D_MODEL ** -0.5)
    mla_q_g = 1.0 + nrm((N_AB_LAYERS, Q_LORA), 0.05)
    w_mla_q_b = nrm((N_AB_LAYERS, Q_LORA, MLA_HEADS, MLA_NOPE + MLA_ROPE), Q_LORA ** -0.5)
    mla_kv_g = 1.0 + nrm((N_AB_LAYERS, KV_LORA), 0.05)
    w_mla_kv_b = nrm((N_AB_LAYERS, KV_LORA, MLA_HEADS, MLA_NOPE + MLA_V), KV_LORA ** -0.5)
    idx_k_g = 1.0 + nrm((N_AB_LAYERS, IDX_DIM), 0.05)
    idx_k_b = nrm((N_AB_LAYERS, IDX_DIM), 0.02)
    w_o_ab = nrm((N_AB_LAYERS, AB_OUT, D_MODEL), AB_OUT ** -0.5 * DEEPNORM_BETA)
    w_in_c = nrm((N_C_LAYERS, D_MODEL, C_IN), D_MODEL ** -0.5)
    w_o_c = nrm((N_C_LAYERS, C_OUT, D_MODEL), C_OUT ** -0.5 * DEEPNORM_BETA)
    ln_g = 1.0 + nrm((DEPTH, 2, D_MODEL), 0.05)
    ln_b = nrm((DEPTH, 2, D_MODEL), 0.02)
    w_gate = nrm((DEPTH, D_MODEL, D_FF), D_MODEL ** -0.5)
    w_up = nrm((DEPTH, D_MODEL, D_FF), D_MODEL ** -0.5)
    w_down = nrm((DEPTH, D_FF, D_MODEL), D_FF ** -0.5 * DEEPNORM_BETA)
    return {"x_prompt": x_prompt, "x_sample": x_sample,
            "cache_mla": cache_mla, "cache_dsa_k": cache_dsa_k, "cache_dsa_v": cache_dsa_v,
            "cache_idx_k": cache_idx_k, "state_win_k": state_win_k, "state_win_v": state_win_v,
            "page_table": page_table,
            "w_in_ab": w_in_ab, "mla_q_g": mla_q_g, "w_mla_q_b": w_mla_q_b, "mla_kv_g": mla_kv_g,
            "w_mla_kv_b": w_mla_kv_b, "idx_k_g": idx_k_g, "idx_k_b": idx_k_b, "w_o_ab": w_o_ab,
            "w_in_c": w_in_c, "w_o_c": w_o_c, "ln_g": ln_g, "ln_b": ln_b,
            "w_gate": w_gate, "w_up": w_up, "w_down": w_down}


def reference(x_prompt, x_sample, cache_mla, cache_dsa_k, cache_dsa_v, cache_idx_k, state_win_k, state_win_v,
              page_table, w_in_ab, mla_q_g, w_mla_q_b, mla_kv_g, w_mla_kv_b, idx_k_g, idx_k_b, w_o_ab,
              w_in_c, w_o_c, ln_g, ln_b, w_gate, w_up, w_down):
    n_tok = x_prompt.shape[1]
    n_dec, n_new = x_sample.shape[:2]
    past = page_table.shape[1] * PAGE_SIZE
    win_buf = state_win_k.shape[2]
    pos_p = jnp.arange(n_tok, dtype=jnp.int32)
    pos_s = past + jnp.arange(n_new, dtype=jnp.int32)
    keys_s = jnp.arange(past + n_new, dtype=jnp.int32)
    hp, hs = x_prompt, x_sample
    mla_p, mla_s, dk_p, dk_s, dv_p, dv_s, ik_p, ik_s, wk_p, wk_s, wv_p, wv_s = ([] for _ in range(12))
    for layer in range(DEPTH):
        if layer % 2 == 0:
            i = layer // 2
            ab_w = (w_in_ab[i], mla_q_g[i], w_mla_q_b[i], mla_kv_g[i], w_mla_kv_b[i], idx_k_g[i], idx_k_b[i])
            mq, mrow, dq, dk, dv, iq, ik, iw = ab_project(hp, pos_p, *ab_w)
            o_lat = map_query_blocks(lambda qb, pb: mla_attend(qb, mrow, pb, pos_p), (mq,), pos_p)
            o_dsa = dsa_prompt(dq, iq, iw, dk, dv, ik, pos_p)
            mix_p = ab_merge(o_lat, o_dsa, w_mla_kv_b[i], w_o_ab[i])
            mla_p.append(mrow); dk_p.append(dk); dv_p.append(dv); ik_p.append(ik)
            mq, mrow, dq, dk, dv, iq, ik, iw = ab_project(hs, pos_s, *ab_w)
            rows_all = jnp.concatenate([cache_mla[i, page_table].reshape(n_dec, past, MLA_CACHE_W), mrow], 1)
            o_lat = mla_attend(mq, rows_all, pos_s, keys_s)
            o_dsa = dsa_sample(dq, iq, iw, dk, dv, ik, cache_dsa_k, cache_dsa_v, cache_idx_k, i, page_table, pos_s)
            mix_s = ab_merge(o_lat, o_dsa, w_mla_kv_b[i], w_o_ab[i])
            mla_s.append(mrow); dk_s.append(dk); dv_s.append(dv); ik_s.append(ik)
        else:
            j = layer // 2
            q, k, v = c_project(hp, pos_p, w_in_c[j])
            o = map_query_blocks(lambda qb, pb: dilated_attend(qb, k, v, pb), (q,), pos_p)
            mix_p = o.reshape(o.shape[0], n_tok, C_OUT) @ w_o_c[j]
            keep = min(WIN_MAX, n_tok)
            wk_p.append(k[:, n_tok - keep:]); wv_p.append(v[:, n_tok - keep:])
            q, k, v = c_project(hs, pos_s, w_in_c[j])
            k_all = jnp.concatenate([state_win_k[j], k], 1)
            v_all = jnp.concatenate([state_win_v[j], v], 1)
            o = dilated_attend(q, k_all, v_all, win_buf + jnp.arange(n_new, dtype=jnp.int32))
            mix_s = o.reshape(n_dec, n_new, C_OUT) @ w_o_c[j]
            wk_s.append(k_all[:, n_new:]); wv_s.append(v_all[:, n_new:])
        hp = deepnorm(hp, mix_p, ln_g[layer, 0], ln_b[layer, 0])
        hp = deepnorm(hp, swiglu(hp, w_gate[layer], w_up[layer], w_down[layer]), ln_g[layer, 1], ln_b[layer, 1])
        hs = deepnorm(hs, mix_s, ln_g[layer, 0], ln_b[layer, 0])
        hs = deepnorm(hs, swiglu(hs, w_gate[layer], w_up[layer], w_down[layer]), ln_g[layer, 1], ln_b[layer, 1])
    new_mla_prompt, new_mla_sample = jnp.stack(mla_p), jnp.stack(mla_s)
    new_dsa_k_prompt, new_dsa_k_sample = jnp.stack(dk_p), jnp.stack(dk_s)
    new_dsa_v_prompt, new_dsa_v_sample = jnp.stack(dv_p), jnp.stack(dv_s)
    new_idx_k_prompt, new_idx_k_sample = jnp.stack(ik_p), jnp.stack(ik_s)
    new_win_k_prompt, new_win_k_sample = jnp.stack(wk_p), jnp.stack(wk_s)
    new_win_v_prompt, new_win_v_sample = jnp.stack(wv_p), jnp.stack(wv_s)
    return (hp, hs, new_mla_prompt, new_mla_sample, new_dsa_k_prompt, new_dsa_k_sample,
            new_dsa_v_prompt, new_dsa_v_sample, new_idx_k_prompt, new_idx_k_sample,
            new_win_k_prompt, new_win_k_sample, new_win_v_prompt, new_win_v_sample)
```

```python
import functools

import numpy as np
import jax
import jax.numpy as jnp
from jax import lax
from jax.experimental import pallas as pl
from jax.experimental.pallas import tpu as pltpu

F32 = jnp.float32
BF16 = jnp.bfloat16
I32 = jnp.int32

D_MODEL = 1024
DEPTH = 4
PAGE_SIZE = 128
HEAD_DIM = 64
ROT_DIM = HEAD_DIM // 4
ROPE_THETA = 500000.0
MLA_HEADS = 8
MLA_NOPE = 64
MLA_ROPE = 32
MLA_V = 64
Q_LORA = 256
KV_LORA = 128
MLA_CACHE_W = KV_LORA + MLA_ROPE
MLA_SCALE = (MLA_NOPE + MLA_ROPE) ** -0.5
DSA_HEADS = 8
DSA_KV_HEADS = 2
DSA_GROUP = DSA_HEADS // DSA_KV_HEADS
IDX_HEADS = 8
IDX_DIM = 64
IDX_ROT = IDX_DIM // 4
IDX_W_SCALE = IDX_HEADS ** -0.5 * IDX_DIM ** -0.5
TOPK_MAX = 256
ATTN_SCALE = HEAD_DIM ** -0.5
DIL_HEADS = 16
DIL_PATTERNS = ((128, 1), (512, 4), (2048, 16))
DIL_KEYS = 128
WIN_MAX = 2048
AB_SPLITS = (Q_LORA, KV_LORA, MLA_ROPE,
             DSA_HEADS * HEAD_DIM, DSA_KV_HEADS * HEAD_DIM, DSA_KV_HEADS * HEAD_DIM,
             IDX_HEADS * IDX_DIM, IDX_DIM, IDX_HEADS)
DEEPNORM_ALPHA = (2 * DEPTH) ** 0.25

LANES = 128
MXU_DIM = 256
VMEM_LIMIT = 52 * 1024 * 1024
NEG = -1e30
INT_MIN = -2 ** 31

_AB_QA, _AB_KVA, _AB_QS, _AB_KS, _AB_VS, _AB_QI, _AB_LAST, _AB_W = 0, 256, 384, 896, 1024, 1152, 1664, 1792
_L_KI, _L_KR, _L_WI = 0, 64, 96


def _cparams(sem):
    return pltpu.CompilerParams(dimension_semantics=sem, vmem_limit_bytes=VMEM_LIMIT)


def _dot(a, b):
    return jnp.dot(a, b, preferred_element_type=F32)


def _dot_t(a, b):
    return lax.dot_general(a, b, (((1,), (1,)), ((), ())), preferred_element_type=F32)


def _roll(x, shift):
    return pltpu.roll(x, shift, 1)


def _rot_tables(pos, offset, span, period, rot_dim):
    half = rot_dim // 2
    inv_freq = ROPE_THETA ** (-(jnp.arange(half, dtype=F32) * 2.0 / rot_dim))
    ang = pos.astype(F32)[:, None] * inv_freq[None, :]
    cos, sin = jnp.cos(ang), jnp.sin(ang)
    m = pos.shape[0]
    one = lambda n: jnp.ones((m, n), F32)
    zero = lambda n: jnp.zeros((m, n), F32)
    rest, tail = period - rot_dim, LANES - offset - span
    reps = span // period
    c = jnp.concatenate([one(offset)] + [cos, cos, one(rest)] * reps + [one(tail)], 1)
    s1 = jnp.concatenate([zero(offset)] + [zero(half), sin, zero(rest)] * reps + [zero(tail)], 1)
    s2 = jnp.concatenate([zero(offset)] + [-sin, zero(half), zero(rest)] * reps + [zero(tail)], 1)
    return c, s1, s2


def _ab_tables(pos):
    ch, s1h, s2h = _rot_tables(pos, 0, LANES, HEAD_DIM, ROT_DIM)
    cr, s1r, s2r = _rot_tables(pos, 0, LANES, MLA_ROPE, MLA_ROPE)
    ci, s1i, s2i = _rot_tables(pos, _L_KI, IDX_DIM, IDX_DIM, IDX_ROT)
    ck, s1k, s2k = _rot_tables(pos, _L_KR, MLA_ROPE, MLA_ROPE, MLA_ROPE)
    return jnp.stack([ch, s1h, s2h, cr, s1r, s2r, ci * ck, s1i, s2i, s1k, s2k])


def _head_tables(pos):
    return jnp.stack(_rot_tables(pos, 0, LANES, HEAD_DIM, ROT_DIM))


def _proj_ab_kernel(x_ref, tab_ref, w_ref, wqn_ref, wqr_ref, bdk_ref, qg_ref, kvg_ref, ikg_ref, ikb_ref,
                    mlaq_ref, rows_ref, dq_ref, dk_ref, dv_ref, iq_ref, ik_ref, iw_ref):
    z = _dot(x_ref[...].astype(BF16), w_ref[...])
    ch, s1h, s2h = tab_ref[0], tab_ref[1], tab_ref[2]

    def rot_h(v):
        return v * ch + _roll(v, ROT_DIM // 2) * s1h + _roll(v, LANES - ROT_DIM // 2) * s2h

    qa = z[:, _AB_QA:_AB_QA + Q_LORA]
    qn = (qa * lax.rsqrt(jnp.mean(qa * qa, -1, keepdims=True) + 1e-6) * qg_ref[...]).astype(BF16)
    q_nope = _dot(qn, wqn_ref[...])
    q_rope = _dot(qn, wqr_ref[...])
    q_lat = _dot(q_nope.astype(BF16), bdk_ref[...])
    cr, s1r, s2r = tab_ref[3], tab_ref[4], tab_ref[5]
    per_blk = LANES // MLA_ROPE
    for c in range(MLA_HEADS // per_blk):
        v = q_rope[:, LANES * c:LANES * (c + 1)]
        vr = v * cr + _roll(v, MLA_ROPE // 2) * s1r + _roll(v, LANES - MLA_ROPE // 2) * s2r
        for hh in range(per_blk):
            mlaq_ref[per_blk * c + hh, :, KV_LORA:MLA_CACHE_W] = (
                vr[:, MLA_ROPE * hh:MLA_ROPE * (hh + 1)].astype(BF16))
    for h in range(MLA_HEADS):
        mlaq_ref[h, :, 0:KV_LORA] = q_lat[:, KV_LORA * h:KV_LORA * (h + 1)].astype(BF16)

    zl = z[:, _AB_LAST:_AB_W]
    lane = lax.broadcasted_iota(I32, zl.shape, 1)
    is_ki = lane < IDX_DIM
    mu = jnp.sum(jnp.where(is_ki, zl, 0.0), -1, keepdims=True) * (1.0 / IDX_DIM)
    dlt = jnp.where(is_ki, zl - mu, 0.0)
    var = jnp.sum(dlt * dlt, -1, keepdims=True) * (1.0 / IDX_DIM)
    y = jnp.where(is_ki, dlt * lax.rsqrt(var + 1e-5) * ikg_ref[...] + ikb_ref[...], zl)
    yr = (y * tab_ref[6]
          + _roll(y, IDX_ROT // 2) * tab_ref[7] + _roll(y, LANES - IDX_ROT // 2) * tab_ref[8]
          + _roll(y, MLA_ROPE // 2) * tab_ref[9] + _roll(y, LANES - MLA_ROPE // 2) * tab_ref[10])
    ik_ref[...] = yr[:, _L_KI:_L_KI + IDX_DIM]
    iw_ref[...] = yr[:, _L_WI:_L_WI + IDX_HEADS] * IDX_W_SCALE
    kva = z[:, _AB_KVA:_AB_KVA + KV_LORA]
    rows_ref[:, 0:KV_LORA] = kva * lax.rsqrt(jnp.mean(kva * kva, -1, keepdims=True) + 1e-6) * kvg_ref[...]
    rows_ref[:, KV_LORA:MLA_CACHE_W] = yr[:, _L_KR:_L_KR + MLA_ROPE]

    for c in range(DSA_HEADS * HEAD_DIM // LANES):
        v = z[:, _AB_QS + LANES * c:_AB_QS + LANES * (c + 1)]
        dq_ref[:, LANES * c:LANES * (c + 1)] = (rot_h(v) * ATTN_SCALE).astype(BF16)
        v = z[:, _AB_QI + LANES * c:_AB_QI + LANES * (c + 1)]
        iq_ref[:, LANES * c:LANES * (c + 1)] = rot_h(v).astype(BF16)
    dk_ref[...] = rot_h(z[:, _AB_KS:_AB_KS + LANES])
    dv_ref[...] = z[:, _AB_VS:_AB_VS + LANES]


def _proj_ab(x, tabs, wts, tm):
    m = x.shape[0]
    w, wqn, wqr, bdk, qg, kvg, ikg, ikb = wts
    row = lambda i: (i, 0)
    full2 = lambda i: (0, 0)
    out_shape = (
        jax.ShapeDtypeStruct((MLA_HEADS, m, MLA_CACHE_W), BF16),
        jax.ShapeDtypeStruct((m, MLA_CACHE_W), F32),
        jax.ShapeDtypeStruct((m, DSA_HEADS * HEAD_DIM), BF16),
        jax.ShapeDtypeStruct((m, DSA_KV_HEADS * HEAD_DIM), F32),
        jax.ShapeDtypeStruct((m, DSA_KV_HEADS * HEAD_DIM), F32),
        jax.ShapeDtypeStruct((m, IDX_HEADS * IDX_DIM), BF16),
        jax.ShapeDtypeStruct((m, IDX_DIM), F32),
        jax.ShapeDtypeStruct((m, IDX_HEADS), F32),
    )
    return pl.pallas_call(
        _proj_ab_kernel,
        grid=(m // tm,),
        in_specs=[
            pl.BlockSpec((tm, D_MODEL), row),
            pl.BlockSpec((tabs.shape[0], tm, LANES), lambda i: (0, i, 0)),
            pl.BlockSpec(w.shape, full2), pl.BlockSpec(wqn.shape, full2), pl.BlockSpec(wqr.shape, full2),
            pl.BlockSpec(bdk.shape, full2), pl.BlockSpec(qg.shape, full2), pl.BlockSpec(kvg.shape, full2),
            pl.BlockSpec(ikg.shape, full2), pl.BlockSpec(ikb.shape, full2),
        ],
        out_specs=(
            pl.BlockSpec((MLA_HEADS, tm, MLA_CACHE_W), lambda i: (0, i, 0)),
            pl.BlockSpec((tm, MLA_CACHE_W), row),
            pl.BlockSpec((tm, DSA_HEADS * HEAD_DIM), row),
            pl.BlockSpec((tm, DSA_KV_HEADS * HEAD_DIM), row),
            pl.BlockSpec((tm, DSA_KV_HEADS * HEAD_DIM), row),
            pl.BlockSpec((tm, IDX_HEADS * IDX_DIM), row),
            pl.BlockSpec((tm, IDX_DIM), row),
            pl.BlockSpec((tm, IDX_HEADS), row),
        ),
        out_shape=out_shape,
        compiler_params=_cparams(("parallel",)),
        name="proj_ab",
    )(x, tabs, w, wqn, wqr, bdk, qg, kvg, ikg, ikb)


def _ab_weights(w_in, q_g, w_q_b, kv_g, w_kv_b, ik_g, ik_b):
    offs = np.cumsum((0,) + AB_SPLITS)
    seg = [w_in[:, offs[k]:offs[k + 1]] for k in range(len(AB_SPLITS))]
    q_a, kv_a, k_r, q_s, k_s, v_s, q_i, k_i, w_i = seg
    pad = jnp.zeros((D_MODEL, LANES - IDX_DIM - MLA_ROPE - IDX_HEADS), w_in.dtype)
    w = jnp.concatenate([q_a, kv_a, q_s, k_s, v_s, q_i, k_i, k_r, w_i, pad], 1).astype(BF16)
    wqn = w_q_b[:, :, :MLA_NOPE].reshape(Q_LORA, MLA_HEADS * MLA_NOPE).astype(BF16)
    wqr = w_q_b[:, :, MLA_NOPE:].reshape(Q_LORA, MLA_HEADS * MLA_ROPE).astype(BF16)
    eye = jnp.eye(MLA_HEADS, dtype=F32)
    bdk = jnp.einsum('chn,hg->hngc', w_kv_b[:, :, :MLA_NOPE], eye).reshape(
        MLA_HEADS * MLA_NOPE, MLA_HEADS * KV_LORA).astype(BF16)
    bdv = jnp.einsum('chv,hg->hcgv', w_kv_b[:, :, MLA_NOPE:], eye).reshape(
        MLA_HEADS * KV_LORA, MLA_HEADS * MLA_V).astype(BF16)
    lpad = jnp.zeros((LANES - IDX_DIM,), F32)
    ikg = jnp.concatenate([ik_g, lpad])[None, :]
    ikb = jnp.concatenate([ik_b, lpad])[None, :]
    return (w, wqn, wqr, bdk, q_g[None, :], kv_g[None, :], ikg, ikb), bdv


def _softmax_step(s, v_bf16, m_s, l_s, acc_s):
    m_prev = m_s[...]
    m_new = jnp.maximum(m_prev, jnp.max(s, -1, keepdims=True))
    alpha = jnp.exp(m_prev - m_new)
    p = jnp.exp(s - m_new)
    l_s[...] = alpha * l_s[...] + jnp.sum(p, -1, keepdims=True)
    acc_s[...] = alpha * acc_s[...] + _dot(p.astype(BF16), v_bf16)
    m_s[...] = m_new


def _softmax_init(m_s, l_s, acc_s):
    m_s[...] = jnp.full(m_s.shape, NEG, F32)
    l_s[...] = jnp.zeros(l_s.shape, F32)
    acc_s[...] = jnp.zeros(acc_s.shape, F32)


def _mla_prompt_kernel(q_ref, rows_ref, bdv_ref, o_ref, m_s, l_s, acc_s, *, tq, tk):
    i = pl.program_id(1)
    q = q_ref[...].reshape(MLA_HEADS * tq, MLA_CACHE_W)
    _softmax_init(m_s, l_s, acc_s)

    def step(j, masked):
        kv = rows_ref[pl.ds(pl.multiple_of(j * tk, tk), tk), :].astype(BF16)
        s = _dot_t(q, kv) * MLA_SCALE
        if masked:
            kpos = j * tk + lax.broadcasted_iota(I32, (1, tk), 1)
            qpos = i * tq + lax.broadcasted_iota(I32, (MLA_HEADS, tq, 1), 1).reshape(MLA_HEADS * tq, 1)
            s = jnp.where(kpos <= qpos, s, NEG)
        _softmax_step(s, kv[:, 0:KV_LORA], m_s, l_s, acc_s)

    n_full = (i * tq) // tk

    def body(j, carry):
        step(j, False)
        return carry

    lax.fori_loop(0, n_full, body, 0)
    step(n_full, True)
    o = acc_s[...] / l_s[...]
    o_tok = jnp.concatenate([o[h * tq:(h + 1) * tq] for h in range(MLA_HEADS)], axis=1).astype(BF16)
    o_ref[...] = _dot(o_tok, bdv_ref[...]).astype(BF16)


def _mla_prompt(mla_q, rows, bdv, n_b, t, tq, tk):
    nq = t // tq
    rows8 = MLA_HEADS * tq
    return pl.pallas_call(
        functools.partial(_mla_prompt_kernel, tq=tq, tk=tk),
        grid=(n_b, nq),
        in_specs=[
            pl.BlockSpec((MLA_HEADS, tq, MLA_CACHE_W), lambda b, i: (0, b * nq + i, 0)),
            pl.BlockSpec((t, MLA_CACHE_W), lambda b, i: (b, 0)),
            pl.BlockSpec(bdv.shape, lambda b, i: (0, 0)),
        ],
        out_specs=pl.BlockSpec((tq, MLA_HEADS * MLA_V), lambda b, i: (b * nq + i, 0)),
        out_shape=jax.ShapeDtypeStruct((n_b * t, MLA_HEADS * MLA_V), BF16),
        scratch_shapes=[pltpu.VMEM((rows8, 1), F32), pltpu.VMEM((rows8, 1), F32),
                        pltpu.VMEM((rows8, KV_LORA), F32)],
        compiler_params=_cparams(("parallel", "parallel")),
        name="mla_prompt",
    )(mla_q, rows, bdv)


def _score_keys(score):
    bits = lax.bitcast_convert_type(score, I32)
    bits = jnp.where(bits == INT_MIN, 0, bits)
    return bits ^ ((bits >> 31) & 0x7FFFFFFF)


def _expand_dsa_q(dq, rows):
    lane = lax.broadcasted_iota(I32, (rows, LANES), 1)
    out = []
    for h in range(DSA_HEADS):
        blk = dq[:, LANES * (h // 2):LANES * (h // 2 + 1)].astype(F32)
        grp = h // DSA_GROUP
        src = blk if (h % 2) == grp else _roll(blk, HEAD_DIM)
        keep = (lane < HEAD_DIM) if grp == 0 else (lane >= HEAD_DIM)
        out.append(jnp.where(keep, src, 0.0).astype(BF16))
    return out


def _stack_idx_q(iq):
    out = []
    for h in range(IDX_HEADS):
        blk = iq[:, LANES * (h // 2):LANES * (h // 2 + 1)].astype(F32)
        src = blk if h % 2 == 0 else _roll(blk, HEAD_DIM)
        out.append(src[:, 0:IDX_DIM].astype(BF16))
    return out


def _collapse_dsa_o(o_heads, rows):
    lane = lax.broadcasted_iota(I32, (rows, LANES), 1)
    cols = []
    for j in range(DSA_HEADS // 2):
        grp = (2 * j) // DSA_GROUP
        a, b = o_heads[2 * j], o_heads[2 * j + 1]
        left = a if grp == 0 else _roll(a, HEAD_DIM)
        right = _roll(b, HEAD_DIM) if grp == 0 else b
        cols.append(jnp.where(lane < HEAD_DIM, left, right))
    return jnp.concatenate(cols, axis=1)


def _tie_bias(k, thr, need, run, tri_ref):
    eqf = jnp.where(k == thr, 1.0, 0.0)
    parts = []
    for blk in range(k.shape[1] // MXU_DIM):
        e = eqf[:, MXU_DIM * blk:MXU_DIM * (blk + 1)]
        parts.append(_dot(e.astype(BF16), tri_ref[...]) + run)
        run = run + jnp.sum(e, -1, keepdims=True)
    pre = parts[0] if len(parts) == 1 else jnp.concatenate(parts, axis=1)
    t_eq = jnp.where(pre < need, 0.0, NEG)
    t_eq = jnp.where(k == INT_MIN, NEG, t_eq)
    bias = jnp.where(k > thr, 0.0, jnp.where(k == thr, t_eq, NEG))
    return bias, run


def _dsa_prompt_kernel(iq_ref, iw_ref, dq_ref, ik_ref, dk_ref, dv_ref, tri_ref, o_ref,
                       key_s, m_s, l_s, acc_s, *, tq, ck, n_sel):
    i = pl.program_id(1)
    rows8 = DSA_HEADS * tq
    n_ck = (i * tq + tq + ck - 1) // ck
    qi = jnp.concatenate(_stack_idx_q(iq_ref[...]), axis=0)
    w = iw_ref[...]
    wcol = jnp.concatenate([w[:, h:h + 1] for h in range(IDX_HEADS)], axis=0)
    qpos = i * tq + lax.broadcasted_iota(I32, (tq, 1), 0)

    def score_chunk(c, carry):
        kc = ik_ref[pl.ds(pl.multiple_of(c * ck, ck), ck), :].astype(BF16)
        s = jnp.maximum(_dot_t(qi, kc), 0.0) * wcol
        score = jnp.sum(s.reshape(IDX_HEADS, tq, ck), axis=0)
        kpos = c * ck + lax.broadcasted_iota(I32, (1, ck), 1)
        key_s[c] = jnp.where(kpos <= qpos, _score_keys(score), INT_MIN)
        return carry

    lax.fori_loop(0, n_ck, score_chunk, 0)

    def count(cand, strict):
        def body(c, acc):
            k = key_s[c]
            hit = jnp.where((k > cand) if strict else (k >= cand), 1.0, 0.0)
            for blk in range(ck // LANES):
                acc = acc + hit[:, LANES * blk:LANES * (blk + 1)]
            return acc
        acc = lax.fori_loop(0, n_ck, body, jnp.zeros((tq, LANES), F32))
        return jnp.sum(acc, -1, keepdims=True)

    def bit_step(b, res):
        cand = res | jnp.left_shift(jnp.int32(1), 31 - b)
        return jnp.where(count(cand ^ INT_MIN, False) >= n_sel, cand, res)

    thr = lax.fori_loop(0, 32, bit_step, jnp.zeros((tq, 1), I32)) ^ INT_MIN
    need = n_sel - count(thr, True)

    q = jnp.concatenate(_expand_dsa_q(dq_ref[...], tq), axis=0)
    _softmax_init(m_s, l_s, acc_s)

    def attend_chunk(c, run):
        bias, run = _tie_bias(key_s[c], thr, need, run, tri_ref)
        start = pl.multiple_of(c * ck, ck)
        kb = dk_ref[pl.ds(start, ck), :].astype(BF16)
        vb = dv_ref[pl.ds(start, ck), :].astype(BF16)
        s = (_dot_t(q, kb).reshape(DSA_HEADS, tq, ck) + bias[None]).reshape(rows8, ck)
        _softmax_step(s, vb, m_s, l_s, acc_s)
        return run

    lax.fori_loop(0, n_ck, attend_chunk, jnp.zeros((tq, 1), F32))
    o = acc_s[...] / l_s[...]
    o_ref[...] = _collapse_dsa_o([o[h * tq:(h + 1) * tq] for h in range(DSA_HEADS)], tq).astype(BF16)


def _dsa_prompt(iq, iw, dq, ik, dk, dv, tri, n_b, t, tq, ck):
    nq = t // tq
    n_sel = min(TOPK_MAX, t // 4)
    rows8 = DSA_HEADS * tq
    qrow = lambda b, i: (b * nq + i, 0)
    brow = lambda b, i: (b, 0)
    return pl.pallas_call(
        functools.partial(_dsa_prompt_kernel, tq=tq, ck=ck, n_sel=n_sel),
        grid=(n_b, nq),
        in_specs=[
            pl.BlockSpec((tq, IDX_HEADS * IDX_DIM), qrow),
            pl.BlockSpec((tq, IDX_HEADS), qrow),
            pl.BlockSpec((tq, DSA_HEADS * HEAD_DIM), qrow),
            pl.BlockSpec((t, IDX_DIM), brow),
            pl.BlockSpec((t, DSA_KV_HEADS * HEAD_DIM), brow),
            pl.BlockSpec((t, DSA_KV_HEADS * HEAD_DIM), brow),
            pl.BlockSpec(tri.shape, lambda b, i: (0, 0)),
        ],
        out_specs=pl.BlockSpec((tq, DSA_HEADS * HEAD_DIM), qrow),
        out_shape=jax.ShapeDtypeStruct((n_b * t, DSA_HEADS * HEAD_DIM), BF16),
        scratch_shapes=[pltpu.VMEM((t // ck, tq, ck), I32),
                        pltpu.VMEM((rows8, 1), F32), pltpu.VMEM((rows8, 1), F32),
                        pltpu.VMEM((rows8, LANES), F32)],
        compiler_params=_cparams(("parallel", "parallel")),
        name="dsa_prompt",
    )(iq, iw, dq, ik, dk, dv, tri)


def _page_copies(pool_ref, layer, pt_ref, b, buf_ref, sem_ref, slot, n_pages):
    return [pltpu.make_async_copy(pool_ref.at[layer, pt_ref[b, p]], buf_ref.at[slot, p], sem_ref.at[slot])
            for p in range(n_pages)]


def _mla_sample_kernel(pt_ref, q_ref, new_ref, pool_ref, o_ref, buf, sem, m_s, l_s, acc_s,
                       *, layer, n_pages, pg_chunk):
    b = pl.program_id(0)
    n_b = pl.num_programs(0)
    slot = b % 2

    @pl.when(b == 0)
    def _():
        for cp in _page_copies(pool_ref, layer, pt_ref, b, buf, sem, slot, n_pages):
            cp.start()

    @pl.when(b + 1 < n_b)
    def _():
        for cp in _page_copies(pool_ref, layer, pt_ref, b + 1, buf, sem, 1 - slot, n_pages):
            cp.start()

    for cp in _page_copies(pool_ref, layer, pt_ref, b, buf, sem, slot, n_pages):
        cp.wait()

    q = q_ref[0]
    _softmax_init(m_s, l_s, acc_s)
    ck = pg_chunk * PAGE_SIZE

    def chunk(c, carry):
        kv = buf[slot, pl.ds(c * pg_chunk, pg_chunk)].reshape(ck, MLA_CACHE_W).astype(BF16)
        _softmax_step(_dot_t(q, kv) * MLA_SCALE, kv[:, 0:KV_LORA], m_s, l_s, acc_s)
        return carry

    lax.fori_loop(0, n_pages // pg_chunk, chunk, 0)
    row = new_ref[0].astype(BF16).astype(F32)
    s_new = jnp.sum(q.astype(F32) * row, -1, keepdims=True) * MLA_SCALE
    m_prev = m_s[...]
    m_new = jnp.maximum(m_prev, s_new)
    alpha = jnp.exp(m_prev - m_new)
    p_new = jnp.exp(s_new - m_new)
    l = alpha * l_s[...] + p_new
    acc = alpha * acc_s[...] + p_new.astype(BF16).astype(F32) * row[:, 0:KV_LORA]
    o_ref[0] = acc / l


def _mla_sample(page_table, q, new_rows, pool, layer, pg_chunk=8):
    n_b, n_pages = page_table.shape
    return pl.pallas_call(
        functools.partial(_mla_sample_kernel, layer=layer, n_pages=n_pages, pg_chunk=pg_chunk),
        grid_spec=pltpu.PrefetchScalarGridSpec(
            num_scalar_prefetch=1,
            grid=(n_b,),
            in_specs=[
                pl.BlockSpec((1, MLA_HEADS, MLA_CACHE_W), lambda b, pt: (b, 0, 0)),
                pl.BlockSpec((1, 1, MLA_CACHE_W), lambda b, pt: (b, 0, 0)),
                pl.BlockSpec(memory_space=pl.ANY),
            ],
            out_specs=pl.BlockSpec((1, MLA_HEADS, KV_LORA), lambda b, pt: (b, 0, 0)),
            scratch_shapes=[pltpu.VMEM((2, n_pages, PAGE_SIZE, MLA_CACHE_W), F32),
                            pltpu.SemaphoreType.DMA((2,)),
                            pltpu.VMEM((MLA_HEADS, 1), F32), pltpu.VMEM((MLA_HEADS, 1), F32),
                            pltpu.VMEM((MLA_HEADS, KV_LORA), F32)],
        ),
        out_shape=jax.ShapeDtypeStruct((n_b, MLA_HEADS, KV_LORA), F32),
        compiler_params=_cparams(("arbitrary",)),
        name="mla_sample",
    )(page_table, q, new_rows, pool)


def _dsa_sample_kernel(pt_ref, iq_ref, iw_ref, dq_ref, ikn_ref, dkn_ref, dvn_ref, tri_ref,
                       pool_ik, pool_k, pool_v, o_ref,
                       buf_ik, buf_k, buf_v, sem, key_s, m_s, l_s, acc_s,
                       *, layer, n_pages, pg_chunk, n_sel):
    b = pl.program_id(0)
    n_b = pl.num_programs(0)
    slot = b % 2

    def copies(bb, sl):
        return (_page_copies(pool_ik, layer, pt_ref, bb, buf_ik, sem.at[0], sl, n_pages)
                + _page_copies(pool_k, layer, pt_ref, bb, buf_k, sem.at[1], sl, n_pages)
                + _page_copies(pool_v, layer, pt_ref, bb, buf_v, sem.at[2], sl, n_pages))

    @pl.when(b == 0)
    def _():
        for cp in copies(b, slot):
            cp.start()

    @pl.when(b + 1 < n_b)
    def _():
        for cp in copies(b + 1, 1 - slot):
            cp.start()

    for cp in copies(b, slot):
        cp.wait()

    ck = pg_chunk * PAGE_SIZE
    n_ck = n_pages // pg_chunk
    n_h = IDX_HEADS
    qi = iq_ref[0]
    wcol = iw_ref[0]

    def score_chunk(c, carry):
        kc = buf_ik[slot, pl.ds(c * pg_chunk, pg_chunk)].reshape(ck, IDX_DIM).astype(BF16)
        s = jnp.maximum(_dot_t(qi, kc), 0.0) * wcol
        tot = jnp.broadcast_to(jnp.sum(s, axis=0, keepdims=True), (n_h, ck))
        key_s[c] = _score_keys(tot)
        return carry

    lax.fori_loop(0, n_ck, score_chunk, 0)
    kn = ikn_ref[0].astype(BF16).astype(F32)
    s_n = jnp.maximum(jnp.sum(qi.astype(F32) * kn, -1, keepdims=True), 0.0) * wcol
    s_n = jnp.broadcast_to(jnp.sum(jnp.broadcast_to(s_n, (n_h, LANES)), axis=0, keepdims=True), (n_h, LANES))
    key_new = _score_keys(s_n)[:, 0:1]

    def count(cand, strict):
        def body(c, acc):
            k = key_s[c]
            hit = jnp.where((k > cand) if strict else (k >= cand), 1.0, 0.0)
            for blk in range(ck // LANES):
                acc = acc + hit[:, LANES * blk:LANES * (blk + 1)]
            return acc
        acc = lax.fori_loop(0, n_ck, body, jnp.zeros((n_h, LANES), F32))
        hit_n = jnp.where((key_new > cand) if strict else (key_new >= cand), 1.0, 0.0)
        return jnp.sum(acc, -1, keepdims=True) + hit_n

    def bit_step(bit, res):
        cand = res | jnp.left_shift(jnp.int32(1), 31 - bit)
        return jnp.where(count(cand ^ INT_MIN, False) >= n_sel, cand, res)

    thr = lax.fori_loop(0, 32, bit_step, jnp.zeros((n_h, 1), I32)) ^ INT_MIN
    need = n_sel - count(thr, True)

    q = dq_ref[0]
    _softmax_init(m_s, l_s, acc_s)

    def attend_chunk(c, run):
        bias, run = _tie_bias(key_s[c], thr, need, run, tri_ref)
        kb = buf_k[slot, pl.ds(c * pg_chunk, pg_chunk)].reshape(ck, LANES).astype(BF16)
        vb = buf_v[slot, pl.ds(c * pg_chunk, pg_chunk)].reshape(ck, LANES).astype(BF16)
        _softmax_step(_dot_t(q, kb) + bias, vb, m_s, l_s, acc_s)
        return run

    run = lax.fori_loop(0, n_ck, attend_chunk, jnp.zeros((n_h, 1), F32))
    bias_new = jnp.where(key_new > thr, 0.0, jnp.where(key_new == thr, jnp.where(run < need, 0.0, NEG), NEG))
    k_n = dkn_ref[0].astype(BF16).astype(F32)
    v_n = dvn_ref[0].astype(BF16).astype(F32)
    s_new = jnp.sum(q.astype(F32) * k_n, -1, keepdims=True) + bias_new
    m_prev = m_s[...]
    m_new = jnp.maximum(m_prev, s_new)
    alpha = jnp.exp(m_prev - m_new)
    p_new = jnp.exp(s_new - m_new)
    l = alpha * l_s[...] + p_new
    acc = alpha * acc_s[...] + p_new.astype(BF16).astype(F32) * v_n
    o = acc / l
    head = lax.broadcasted_iota(I32, (n_h, LANES), 0)
    o = jnp.where(head < DSA_GROUP, o, _roll(o, HEAD_DIM))
    o_ref[0] = o[:, 0:HEAD_DIM]


def _dsa_sample(page_table, iq, iw, dq, ik_new, dk_new, dv_new, tri, pool_ik, pool_k, pool_v, layer, pg_chunk=8):
    n_b, n_pages = page_table.shape
    n_sel = min(TOPK_MAX, (n_pages * PAGE_SIZE + 1) // 4)
    ck = pg_chunk * PAGE_SIZE
    n_ck = n_pages // pg_chunk
    per_b = lambda b, pt: (b, 0, 0)
    any_spec = pl.BlockSpec(memory_space=pl.ANY)
    kvw = DSA_KV_HEADS * HEAD_DIM
    return pl.pallas_call(
        functools.partial(_dsa_sample_kernel, layer=layer, n_pages=n_pages, pg_chunk=pg_chunk, n_sel=n_sel),
        grid_spec=pltpu.PrefetchScalarGridSpec(
            num_scalar_prefetch=1,
            grid=(n_b,),
            in_specs=[
                pl.BlockSpec((1, IDX_HEADS, IDX_DIM), per_b),
                pl.BlockSpec((1, IDX_HEADS, 1), per_b),
                pl.BlockSpec((1, DSA_HEADS, LANES), per_b),
                pl.BlockSpec((1, 1, IDX_DIM), per_b),
                pl.BlockSpec((1, 1, kvw), per_b),
                pl.BlockSpec((1, 1, kvw), per_b),
                pl.BlockSpec(tri.shape, lambda b, pt: (0, 0)),
                any_spec, any_spec, any_spec,
            ],
            out_specs=pl.BlockSpec((1, DSA_HEADS, HEAD_DIM), per_b),
            scratch_shapes=[pltpu.VMEM((2, n_pages, PAGE_SIZE, IDX_DIM), F32),
                            pltpu.VMEM((2, n_pages, PAGE_SIZE, kvw), F32),
                            pltpu.VMEM((2, n_pages, PAGE_SIZE, kvw), F32),
                            pltpu.SemaphoreType.DMA((3, 2)),
                            pltpu.VMEM((n_ck, IDX_HEADS, ck), I32),
                            pltpu.VMEM((DSA_HEADS, 1), F32), pltpu.VMEM((DSA_HEADS, 1), F32),
                            pltpu.VMEM((DSA_HEADS, LANES), F32)],
        ),
        out_shape=jax.ShapeDtypeStruct((n_b, DSA_HEADS, HEAD_DIM), F32),
        compiler_params=_cparams(("arbitrary",)),
        name="dsa_sample",
    )(page_table, iq, iw, dq, ik_new, dk_new, dv_new, tri, pool_ik, pool_k, pool_v)


def _proj_c_kernel(x_ref, tab_ref, w_ref, q_ref, k_ref, v_ref):
    z = _dot(x_ref[...].astype(BF16), w_ref[...])
    ch, s1h, s2h = tab_ref[0], tab_ref[1], tab_ref[2]
    width = DIL_HEADS * HEAD_DIM
    for c in range(width // LANES):
        v = z[:, LANES * c:LANES * (c + 1)]
        q_ref[:, LANES * c:LANES * (c + 1)] = (
            v * ch + _roll(v, ROT_DIM // 2) * s1h + _roll(v, LANES - ROT_DIM // 2) * s2h) * ATTN_SCALE
        v = z[:, width + LANES * c:width + LANES * (c + 1)]
        k_ref[:, LANES * c:LANES * (c + 1)] = (
            v * ch + _roll(v, ROT_DIM // 2) * s1h + _roll(v, LANES - ROT_DIM // 2) * s2h)
    v_ref[...] = z[:, 2 * width:3 * width]


def _proj_c(x, tabs, w, tm):
    m = x.shape[0]
    width = DIL_HEADS * HEAD_DIM
    row = lambda i: (i, 0)
    sds = jax.ShapeDtypeStruct((m, width), F32)
    return pl.pallas_call(
        _proj_c_kernel,
        grid=(m // tm,),
        in_specs=[pl.BlockSpec((tm, D_MODEL), row),
                  pl.BlockSpec((3, tm, LANES), lambda i: (0, i, 0)),
                  pl.BlockSpec(w.shape, lambda i: (0, 0))],
        out_specs=(pl.BlockSpec((tm, width), row),) * 3,
        out_shape=(sds, sds, sds),
        compiler_params=_cparams(("parallel",)),
        name="proj_c",
    )(x, tabs, w)


def _dil_prompt_kernel(q_ref, k_ref, v_ref, o_ref, og_s, lse_s, *, t, tu, mc):
    lane = lax.broadcasted_iota(I32, (tu, LANES), 1)
    low = lane < HEAD_DIM
    row2 = lax.broadcasted_iota(I32, (2, tu, 1), 1).reshape(2 * tu, 1)

    for g, (window, dil) in enumerate(DIL_PATTERNS):
        n_ub = t // dil // tu
        back = window // dil

        def unit(u, carry, g=g, dil=dil, n_ub=n_ub, back=back):
            r = u // n_ub
            ub = u % n_ub
            q_start = r + dil * tu * ub
            qf = q_ref[pl.ds(q_start, tu, stride=dil), :]
            q2 = jnp.concatenate([jnp.where(low, qf, 0.0), jnp.where(low, 0.0, qf)], axis=0).astype(BF16)
            u0 = jnp.maximum(ub * tu - back, 0)
            k_start = r + dil * u0
            kb = k_ref[pl.ds(k_start, 2 * tu, stride=dil), :].astype(BF16)
            vb = v_ref[pl.ds(k_start, 2 * tu, stride=dil), :].astype(BF16)
            s = _dot_t(q2, kb)
            uq = ub * tu + row2
            uk = u0 + lax.broadcasted_iota(I32, (1, 2 * tu), 1)
            s = jnp.where((uk <= uq) & (uk >= uq - back), s, NEG)
            m = jnp.max(s, -1, keepdims=True)
            p = jnp.exp(s - m)
            l = jnp.sum(p, -1, keepdims=True)
            o = _dot(p.astype(BF16), vb) / l
            lse = m + jnp.log(l)
            og_s[g, pl.ds(q_start, tu, stride=dil), :] = jnp.where(low, o[0:tu], o[tu:2 * tu])
            lse_s[g, pl.ds(q_start, tu, stride=dil), :] = jnp.where(
                low, jnp.broadcast_to(lse[0:tu], (tu, LANES)), jnp.broadcast_to(lse[tu:2 * tu], (tu, LANES)))
            return carry

        lax.fori_loop(0, dil * n_ub, unit, 0)

    def merge(c, carry):
        sl = pl.ds(pl.multiple_of(c * mc, mc), mc)
        l0, l1, l2 = lse_s[0, sl, :], lse_s[1, sl, :], lse_s[2, sl, :]
        mx = jnp.maximum(jnp.maximum(l0, l1), l2)
        w0, w1, w2 = jnp.exp(l0 - mx), jnp.exp(l1 - mx), jnp.exp(l2 - mx)
        o = (w0 * og_s[0, sl, :] + w1 * og_s[1, sl, :] + w2 * og_s[2, sl, :]) / (w0 + w1 + w2)
        o_ref[sl, :] = o.astype(BF16)
        return carry

    lax.fori_loop(0, t // mc, merge, 0)


def _dil_prompt(q, k, v, n_b, t, tu=128, mc=512):
    n_pair = DIL_HEADS * HEAD_DIM // LANES
    spec = pl.BlockSpec((t, LANES), lambda b, p: (b, p))
    return pl.pallas_call(
        functools.partial(_dil_prompt_kernel, t=t, tu=tu, mc=mc),
        grid=(n_b, n_pair),
        in_specs=[spec, spec, spec],
        out_specs=spec,
        out_shape=jax.ShapeDtypeStruct((n_b * t, DIL_HEADS * HEAD_DIM), BF16),
        scratch_shapes=[pltpu.VMEM((len(DIL_PATTERNS), t, LANES), F32),
                        pltpu.VMEM((len(DIL_PATTERNS), t, LANES), F32)],
        compiler_params=_cparams(("parallel", "parallel")),
        name="dil_prompt",
    )(q, k, v)


def _dil_sample_kernel(q_ref, kn_ref, vn_ref, k0_ref, k1_ref, k2_ref, v0_ref, v1_ref, v2_ref, o_ref):
    width = DIL_HEADS * HEAD_DIM
    q = q_ref[0]
    head_of_lane = lax.broadcasted_iota(I32, (DIL_HEADS, width), 1) // HEAD_DIM
    own = head_of_lane == lax.broadcasted_iota(I32, (DIL_HEADS, width), 0)
    qe = jnp.where(own, jnp.broadcast_to(q, (DIL_HEADS, width)), 0.0)
    qe16 = qe.astype(BF16)
    kb = jnp.concatenate([k0_ref[0, 0], k1_ref[0, 0], k2_ref[0, 0]], axis=0).astype(BF16)
    vb = jnp.concatenate([v0_ref[0, 0], v1_ref[0, 0], v2_ref[0, 0]], axis=0).astype(BF16)
    s = _dot_t(qe16, kb)
    k_n = kn_ref[0].astype(BF16).astype(F32)
    v_n = vn_ref[0].astype(BF16).astype(F32)
    s_new = jnp.sum(qe16.astype(F32) * k_n, -1, keepdims=True)
    m = jnp.maximum(jnp.max(s, -1, keepdims=True), s_new)
    p = jnp.exp(s - m)
    p_new = jnp.exp(s_new - m)
    n_pat = float(len(DIL_PATTERNS))
    l = jnp.sum(p, -1, keepdims=True) + n_pat * p_new
    acc = _dot(p.astype(BF16), vb) + n_pat * p_new.astype(BF16).astype(F32) * v_n
    o = jnp.where(own, acc / l, 0.0)
    o_ref[0] = jnp.sum(o, axis=0, keepdims=True)


def _dil_sample(q, k_new, v_new, state_k, state_v, layer):
    n_b = q.shape[0]
    width = DIL_HEADS * HEAD_DIM
    n_l = state_k.shape[0]
    per_b = pl.BlockSpec((1, 1, width), lambda b: (b, 0, 0))
    views, specs = [], []
    for st in (state_k, state_v):
        for window, dil in DIL_PATTERNS:
            views.append(st.reshape(n_l, n_b, WIN_MAX // dil, dil * width))
            blk = (WIN_MAX - window) // dil // DIL_KEYS
            specs.append(pl.BlockSpec((1, 1, DIL_KEYS, width), lambda b, blk=blk: (layer, b, blk, 0)))
    return pl.pallas_call(
        _dil_sample_kernel,
        grid=(n_b,),
        in_specs=[per_b, per_b, per_b] + specs,
        out_specs=per_b,
        out_shape=jax.ShapeDtypeStruct((n_b, 1, width), F32),
        compiler_params=_cparams(("parallel",)),
        name="dil_sample",
    )(q, k_new, v_new, *views)


def _mm_kernel(a_ref, w_ref, o_ref):
    o_ref[...] = _dot(a_ref[...].astype(BF16), w_ref[...]).astype(o_ref.dtype)


def _mm(a, w, out_dtype, tm):
    m, k = a.shape
    n = w.shape[1]
    return pl.pallas_call(
        _mm_kernel,
        grid=(m // tm,),
        in_specs=[pl.BlockSpec((tm, k), lambda i: (i, 0)), pl.BlockSpec((k, n), lambda i: (0, 0))],
        out_specs=pl.BlockSpec((tm, n), lambda i: (i, 0)),
        out_shape=jax.ShapeDtypeStruct((m, n), out_dtype),
        compiler_params=_cparams(("parallel",)),
        name="mm",
    )(a, w)


def _mm_ln_kernel(*refs, n_in):
    x_ref = refs[0]
    a_refs = refs[1:1 + n_in]
    w_refs = refs[1 + n_in:1 + 2 * n_in]
    g_ref, b_ref, o_ref = refs[1 + 2 * n_in:]
    acc = DEEPNORM_ALPHA * x_ref[...]
    for a_ref, w_ref in zip(a_refs, w_refs):
        acc = acc + _dot(a_ref[...].astype(BF16), w_ref[...])
    mu = jnp.mean(acc, -1, keepdims=True)
    dlt = acc - mu
    var = jnp.mean(dlt * dlt, -1, keepdims=True)
    o_ref[...] = dlt * lax.rsqrt(var + 1e-5) * g_ref[...] + b_ref[...]


def _mm_ln(x, a_list, w_list, g, b, tm):
    m = x.shape[0]
    n_in = len(a_list)
    row = lambda i: (i, 0)
    fix = lambda i: (0, 0)
    in_specs = ([pl.BlockSpec((tm, D_MODEL), row)]
                + [pl.BlockSpec((tm, a.shape[1]), row) for a in a_list]
                + [pl.BlockSpec(w.shape, fix) for w in w_list]
                + [pl.BlockSpec((1, D_MODEL), fix), pl.BlockSpec((1, D_MODEL), fix)])
    return pl.pallas_call(
        functools.partial(_mm_ln_kernel, n_in=n_in),
        grid=(m // tm,),
        in_specs=in_specs,
        out_specs=pl.BlockSpec((tm, D_MODEL), row),
        out_shape=jax.ShapeDtypeStruct((m, D_MODEL), F32),
        compiler_params=_cparams(("parallel",)),
        name="mm_ln",
    )(x, *a_list, *w_list, g[None, :], b[None, :])


def _swiglu_up_kernel(x_ref, wg_ref, wu_ref, o_ref):
    x = x_ref[...].astype(BF16)
    gate = _dot(x, wg_ref[...])
    up = _dot(x, wu_ref[...])
    o_ref[...] = (gate * jax.nn.sigmoid(gate) * up).astype(BF16)


def _swiglu_up(x, wg, wu, tm):
    m = x.shape[0]
    d_ff = wg.shape[1]
    return pl.pallas_call(
        _swiglu_up_kernel,
        grid=(m // tm,),
        in_specs=[pl.BlockSpec((tm, D_MODEL), lambda i: (i, 0)),
                  pl.BlockSpec(wg.shape, lambda i: (0, 0)), pl.BlockSpec(wu.shape, lambda i: (0, 0))],
        out_specs=pl.BlockSpec((tm, d_ff), lambda i: (i, 0)),
        out_shape=jax.ShapeDtypeStruct((m, d_ff), BF16),
        compiler_params=_cparams(("parallel",)),
        name="swiglu_up",
    )(x, wg, wu)


def _row_tile(m):
    for tm in (256, 128):
        if m % tm == 0:
            return tm
    return m


def kernel(x_prompt, x_sample, cache_mla, cache_dsa_k, cache_dsa_v, cache_idx_k, state_win_k, state_win_v,
           page_table, w_in_ab, mla_q_g, w_mla_q_b, mla_kv_g, w_mla_kv_b, idx_k_g, idx_k_b, w_o_ab,
           w_in_c, w_o_c, ln_g, ln_b, w_gate, w_up, w_down):
    n_b, t = x_prompt.shape[:2]
    n_dec, n_new = x_sample.shape[:2]
    n_pages = page_table.shape[1]
    past = n_pages * PAGE_SIZE
    win_buf = state_win_k.shape[2]
    assert n_new == 1 and win_buf == WIN_MAX and t % (16 * 256) == 0 and n_pages % 8 == 0
    mp, ms = n_b * t, n_dec
    tm_p, tm_s = _row_tile(mp), _row_tile(ms)

    pos_p = jnp.tile(jnp.arange(t, dtype=I32), n_b)
    pos_s = jnp.full((ms,), past, I32)
    tab_ab_p, tab_ab_s = _ab_tables(pos_p), _ab_tables(pos_s)
    tab_h_p, tab_h_s = _head_tables(pos_p), _head_tables(pos_s)
    tri = jnp.triu(jnp.ones((MXU_DIM, MXU_DIM), F32), 1).astype(BF16)
    pool_kw = DSA_KV_HEADS * HEAD_DIM
    pool_k = cache_dsa_k.reshape(cache_dsa_k.shape[:3] + (pool_kw,))
    pool_v = cache_dsa_v.reshape(cache_dsa_v.shape[:3] + (pool_kw,))

    hp = x_prompt.reshape(mp, D_MODEL)
    hs = x_sample.reshape(ms, D_MODEL)
    outs = {k: [] for k in ("mla_p", "mla_s", "dk_p", "dk_s", "dv_p", "dv_s", "ik_p", "ik_s",
                            "wk_p", "wk_s", "wv_p", "wv_s")}
    for layer in range(DEPTH):
        if layer % 2 == 0:
            i = layer // 2
            wts, bdv = _ab_weights(w_in_ab[i], mla_q_g[i], w_mla_q_b[i], mla_kv_g[i], w_mla_kv_b[i],
                                   idx_k_g[i], idx_k_b[i])
            w_o = w_o_ab[i].astype(BF16)
            w_o_mla, w_o_dsa = w_o[:MLA_HEADS * MLA_V], w_o[MLA_HEADS * MLA_V:]
            mla_q, rows, dq, dk, dv, iq, ik, iw = _proj_ab(hp, tab_ab_p, wts, tm_p)
            o_mla = _mla_prompt(mla_q, rows, bdv, n_b, t, tq=128, tk=512)
            o_dsa = _dsa_prompt(iq, iw, dq, ik, dk, dv, tri, n_b, t, tq=128, ck=512)
            mix_in_p = ([o_mla, o_dsa], [w_o_mla, w_o_dsa])
            outs["mla_p"].append(rows.reshape(n_b, t, MLA_CACHE_W))
            outs["dk_p"].append(dk.reshape(n_b, t, DSA_KV_HEADS, HEAD_DIM))
            outs["dv_p"].append(dv.reshape(n_b, t, DSA_KV_HEADS, HEAD_DIM))
            outs["ik_p"].append(ik.reshape(n_b, t, IDX_DIM))
            mla_q, rows, dq, dk, dv, iq, ik, iw = _proj_ab(hs, tab_ab_s, wts, tm_s)
            o_lat = _mla_sample(page_table, jnp.transpose(mla_q, (1, 0, 2)), rows[:, None, :], cache_mla, i)
            o_mla = _mm(o_lat.reshape(ms, MLA_HEADS * KV_LORA), bdv, BF16, tm_s)
            iq_h = iq.reshape(ms, IDX_HEADS, IDX_DIM)
            dq_h = dq.reshape(ms, DSA_KV_HEADS, DSA_GROUP, HEAD_DIM)
            zero = jnp.zeros_like(dq_h[:, 0])
            dq_e = jnp.concatenate([jnp.concatenate([dq_h[:, 0], zero], -1),
                                    jnp.concatenate([zero, dq_h[:, 1]], -1)], 1)
            o_e = _dsa_sample(page_table, iq_h, iw[:, :, None], dq_e, ik[:, None, :], dk[:, None, :],
                              dv[:, None, :], tri, cache_idx_k, pool_k, pool_v, i)
            mix_in_s = ([o_mla, o_e.reshape(ms, DSA_HEADS * HEAD_DIM)], [w_o_mla, w_o_dsa])
            outs["mla_s"].append(rows.reshape(n_dec, n_new, MLA_CACHE_W))
            outs["dk_s"].append(dk.reshape(n_dec, n_new, DSA_KV_HEADS, HEAD_DIM))
            outs["dv_s"].append(dv.reshape(n_dec, n_new, DSA_KV_HEADS, HEAD_DIM))
            outs["ik_s"].append(ik.reshape(n_dec, n_new, IDX_DIM))
        else:
            j = layer // 2
            w_c = w_in_c[j].astype(BF16)
            w_o = w_o_c[j].astype(BF16)
            q, k, v = _proj_c(hp, tab_h_p, w_c, tm_p)
            o = _dil_prompt(q, k, v, n_b, t)
            mix_in_p = ([o], [w_o])
            keep = min(WIN_MAX, t)
            outs["wk_p"].append(k.reshape(n_b, t, DIL_HEADS, HEAD_DIM)[:, t - keep:])
            outs["wv_p"].append(v.reshape(n_b, t, DIL_HEADS, HEAD_DIM)[:, t - keep:])
            q, k, v = _proj_c(hs, tab_h_s, w_c, tm_s)
            o = _dil_sample(q[:, None, :], k[:, None, :], v[:, None, :], state_win_k, state_win_v, j)
            mix_in_s = ([o.reshape(ms, DIL_HEADS * HEAD_DIM)], [w_o])
            k_new = k.reshape(n_dec, n_new, DIL_HEADS, HEAD_DIM)
            v_new = v.reshape(n_dec, n_new, DIL_HEADS, HEAD_DIM)
            outs["wk_s"].append(jnp.concatenate([state_win_k[j][:, n_new:], k_new], 1))
            outs["wv_s"].append(jnp.concatenate([state_win_v[j][:, n_new:], v_new], 1))
        wg, wu, wd = w_gate[layer].astype(BF16), w_up[layer].astype(BF16), w_down[layer].astype(BF16)
        hp = _mm_ln(hp, *mix_in_p, ln_g[layer, 0], ln_b[layer, 0], tm_p)
        hp = _mm_ln(hp, [_swiglu_up(hp, wg, wu, tm_p)], [wd], ln_g[layer, 1], ln_b[layer, 1], tm_p)
        hs = _mm_ln(hs, *mix_in_s, ln_g[layer, 0], ln_b[layer, 0], tm_s)
        hs = _mm_ln(hs, [_swiglu_up(hs, wg, wu, tm_s)], [wd], ln_g[layer, 1], ln_b[layer, 1], tm_s)
    st = {k: jnp.stack(v) for k, v in outs.items()}
    return (hp.reshape(n_b, t, D_MODEL), hs.reshape(n_dec, n_new, D_MODEL),
            st["mla_p"], st["mla_s"], st["dk_p"], st["dk_s"], st["dv_p"], st["dv_s"],
            st["ik_p"], st["ik_s"], st["wk_p"], st["wk_s"], st["wv_p"], st["wv_s"])
```

```python
import functools

import numpy as np
import jax
import jax.numpy as jnp
from jax import lax
from jax.experimental import pallas as pl
from jax.experimental.pallas import tpu as pltpu

F32 = jnp.float32
BF16 = jnp.bfloat16
I32 = jnp.int32

D_MODEL = 1024
DEPTH = 4
PAGE_SIZE = 128
HEAD_DIM = 64
ROT_DIM = HEAD_DIM // 4
ROPE_THETA = 500000.0
MLA_HEADS = 8
MLA_NOPE = 64
MLA_ROPE = 32
MLA_V = 64
Q_LORA = 256
KV_LORA = 128
MLA_CACHE_W = KV_LORA + MLA_ROPE
MLA_SCALE = (MLA_NOPE + MLA_ROPE) ** -0.5
DSA_HEADS = 8
DSA_KV_HEADS = 2
DSA_GROUP = DSA_HEADS // DSA_KV_HEADS
IDX_HEADS = 8
IDX_DIM = 64
IDX_ROT = IDX_DIM // 4
IDX_W_SCALE = IDX_HEADS ** -0.5 * IDX_DIM ** -0.5
TOPK_MAX = 256
ATTN_SCALE = HEAD_DIM ** -0.5
DIL_HEADS = 16
DIL_PATTERNS = ((128, 1), (512, 4), (2048, 16))
DIL_KEYS = 128
WIN_MAX = 2048
AB_SPLITS = (Q_LORA, KV_LORA, MLA_ROPE,
             DSA_HEADS * HEAD_DIM, DSA_KV_HEADS * HEAD_DIM, DSA_KV_HEADS * HEAD_DIM,
             IDX_HEADS * IDX_DIM, IDX_DIM, IDX_HEADS)
DEEPNORM_ALPHA = (2 * DEPTH) ** 0.25

LANES = 128
MXU_DIM = 256
VMEM_LIMIT = 52 * 1024 * 1024
NEG = -1e30
INT_MIN = -2 ** 31

_AB_QA, _AB_KVA, _AB_QS, _AB_KS, _AB_VS, _AB_QI, _AB_LAST, _AB_W = 0, 256, 384, 896, 1024, 1152, 1664, 1792
_L_KI, _L_KR, _L_WI = 0, 64, 96


def _cparams(sem):
    return pltpu.CompilerParams(dimension_semantics=sem, vmem_limit_bytes=VMEM_LIMIT)


def _dot(a, b):
    return jnp.dot(a, b, preferred_element_type=F32)


def _dot_t(a, b):
    return lax.dot_general(a, b, (((1,), (1,)), ((), ())), preferred_element_type=F32)


def _roll(x, shift):
    return pltpu.roll(x, shift, 1)


def _rot_tables(pos, offset, span, period, rot_dim):
    half = rot_dim // 2
    inv_freq = ROPE_THETA ** (-(jnp.arange(half, dtype=F32) * 2.0 / rot_dim))
    ang = pos.astype(F32)[:, None] * inv_freq[None, :]
    cos, sin = jnp.cos(ang), jnp.sin(ang)
    m = pos.shape[0]
    one = lambda n: jnp.ones((m, n), F32)
    zero = lambda n: jnp.zeros((m, n), F32)
    rest, tail = period - rot_dim, LANES - offset - span
    reps = span // period
    c = jnp.concatenate([one(offset)] + [cos, cos, one(rest)] * reps + [one(tail)], 1)
    s1 = jnp.concatenate([zero(offset)] + [zero(half), sin, zero(rest)] * reps + [zero(tail)], 1)
    s2 = jnp.concatenate([zero(offset)] + [-sin, zero(half), zero(rest)] * reps + [zero(tail)], 1)
    return c, s1, s2


def _ab_tables(pos):
    ch, s1h, s2h = _rot_tables(pos, 0, LANES, HEAD_DIM, ROT_DIM)
    cr, s1r, s2r = _rot_tables(pos, 0, LANES, MLA_ROPE, MLA_ROPE)
    ci, s1i, s2i = _rot_tables(pos, _L_KI, IDX_DIM, IDX_DIM, IDX_ROT)
    ck, s1k, s2k = _rot_tables(pos, _L_KR, MLA_ROPE, MLA_ROPE, MLA_ROPE)
    return jnp.stack([ch, s1h, s2h, cr, s1r, s2r, ci * ck, s1i, s2i, s1k, s2k])


def _head_tables(pos):
    return jnp.stack(_rot_tables(pos, 0, LANES, HEAD_DIM, ROT_DIM))


def _proj_ab_kernel(x_ref, tab_ref, w_ref, wqn_ref, wqr_ref, bdk_ref, qg_ref, kvg_ref, ikg_ref, ikb_ref,
                    mlaq_ref, rows_ref, dq_ref, dk_ref, dv_ref, iq_ref, ik_ref, iw_ref):
    z = _dot(x_ref[...].astype(BF16), w_ref[...])
    ch, s1h, s2h = tab_ref[0], tab_ref[1], tab_ref[2]

    def rot_h(v):
        return v * ch + _roll(v, ROT_DIM // 2) * s1h + _roll(v, LANES - ROT_DIM // 2) * s2h

    qa = z[:, _AB_QA:_AB_QA + Q_LORA]
    qn = (qa * lax.rsqrt(jnp.mean(qa * qa, -1, keepdims=True) + 1e-6) * qg_ref[...]).astype(BF16)
    q_nope = _dot(qn, wqn_ref[...])
    q_rope = _dot(qn, wqr_ref[...])
    q_lat = _dot(q_nope.astype(BF16), bdk_ref[...])
    cr, s1r, s2r = tab_ref[3], tab_ref[4], tab_ref[5]
    per_blk = LANES // MLA_ROPE
    for c in range(MLA_HEADS // per_blk):
        v = q_rope[:, LANES * c:LANES * (c + 1)]
        vr = v * cr + _roll(v, MLA_ROPE // 2) * s1r + _roll(v, LANES - MLA_ROPE // 2) * s2r
        for hh in range(per_blk):
            mlaq_ref[per_blk * c + hh, :, KV_LORA:MLA_CACHE_W] = (
                vr[:, MLA_ROPE * hh:MLA_ROPE * (hh + 1)].astype(BF16))
    for h in range(MLA_HEADS):
        mlaq_ref[h, :, 0:KV_LORA] = q_lat[:, KV_LORA * h:KV_LORA * (h + 1)].astype(BF16)

    zl = z[:, _AB_LAST:_AB_W]
    lane = lax.broadcasted_iota(I32, zl.shape, 1)
    is_ki = lane < IDX_DIM
    mu = jnp.sum(jnp.where(is_ki, zl, 0.0), -1, keepdims=True) * (1.0 / IDX_DIM)
    dlt = jnp.where(is_ki, zl - mu, 0.0)
    var = jnp.sum(dlt * dlt, -1, keepdims=True) * (1.0 / IDX_DIM)
    y = jnp.where(is_ki, dlt * lax.rsqrt(var + 1e-5) * ikg_ref[...] + ikb_ref[...], zl)
    yr = (y * tab_ref[6]
          + _roll(y, IDX_ROT // 2) * tab_ref[7] + _roll(y, LANES - IDX_ROT // 2) * tab_ref[8]
          + _roll(y, MLA_ROPE // 2) * tab_ref[9] + _roll(y, LANES - MLA_ROPE // 2) * tab_ref[10])
    ik_ref[...] = yr[:, _L_KI:_L_KI + IDX_DIM]
    iw_ref[...] = yr[:, _L_WI:_L_WI + IDX_HEADS] * IDX_W_SCALE
    kva = z[:, _AB_KVA:_AB_KVA + KV_LORA]
    rows_ref[:, 0:KV_LORA] = kva * lax.rsqrt(jnp.mean(kva * kva, -1, keepdims=True) + 1e-6) * kvg_ref[...]
    rows_ref[:, KV_LORA:MLA_CACHE_W] = yr[:, _L_KR:_L_KR + MLA_ROPE]

    for c in range(DSA_HEADS * HEAD_DIM // LANES):
        v = z[:, _AB_QS + LANES * c:_AB_QS + LANES * (c + 1)]
        dq_ref[:, LANES * c:LANES * (c + 1)] = (rot_h(v) * ATTN_SCALE).astype(BF16)
        v = z[:, _AB_QI + LANES * c:_AB_QI + LANES * (c + 1)]
        iq_ref[:, LANES * c:LANES * (c + 1)] = rot_h(v).astype(BF16)
    dk_ref[...] = rot_h(z[:, _AB_KS:_AB_KS + LANES])
    dv_ref[...] = z[:, _AB_VS:_AB_VS + LANES]


def _proj_ab(x, tabs, wts, tm):
    m = x.shape[0]
    w, wqn, wqr, bdk, qg, kvg, ikg, ikb = wts
    row = lambda i: (i, 0)
    full2 = lambda i: (0, 0)
    out_shape = (
        jax.ShapeDtypeStruct((MLA_HEADS, m, MLA_CACHE_W), BF16),
        jax.ShapeDtypeStruct((m, MLA_CACHE_W), F32),
        jax.ShapeDtypeStruct((m, DSA_HEADS * HEAD_DIM), BF16),
        jax.ShapeDtypeStruct((m, DSA_KV_HEADS * HEAD_DIM), F32),
        jax.ShapeDtypeStruct((m, DSA_KV_HEADS * HEAD_DIM), F32),
        jax.ShapeDtypeStruct((m, IDX_HEADS * IDX_DIM), BF16),
        jax.ShapeDtypeStruct((m, IDX_DIM), F32),
        jax.ShapeDtypeStruct((m, IDX_HEADS), F32),
    )
    return pl.pallas_call(
        _proj_ab_kernel,
        grid=(m // tm,),
        in_specs=[
            pl.BlockSpec((tm, D_MODEL), row),
            pl.BlockSpec((tabs.shape[0], tm, LANES), lambda i: (0, i, 0)),
            pl.BlockSpec(w.shape, full2), pl.BlockSpec(wqn.shape, full2), pl.BlockSpec(wqr.shape, full2),
            pl.BlockSpec(bdk.shape, full2), pl.BlockSpec(qg.shape, full2), pl.BlockSpec(kvg.shape, full2),
            pl.BlockSpec(ikg.shape, full2), pl.BlockSpec(ikb.shape, full2),
        ],
        out_specs=(
            pl.BlockSpec((MLA_HEADS, tm, MLA_CACHE_W), lambda i: (0, i, 0)),
            pl.BlockSpec((tm, MLA_CACHE_W), row),
            pl.BlockSpec((tm, DSA_HEADS * HEAD_DIM), row),
            pl.BlockSpec((tm, DSA_KV_HEADS * HEAD_DIM), row),
            pl.BlockSpec((tm, DSA_KV_HEADS * HEAD_DIM), row),
            pl.BlockSpec((tm, IDX_HEADS * IDX_DIM), row),
            pl.BlockSpec((tm, IDX_DIM), row),
            pl.BlockSpec((tm, IDX_HEADS), row),
        ),
        out_shape=out_shape,
        compiler_params=_cparams(("parallel",)),
        name="proj_ab",
    )(x, tabs, w, wqn, wqr, bdk, qg, kvg, ikg, ikb)


def _ab_weights(w_in, q_g, w_q_b, kv_g, w_kv_b, ik_g, ik_b):
    offs = np.cumsum((0,) + AB_SPLITS)
    seg = [w_in[:, offs[k]:offs[k + 1]] for k in range(len(AB_SPLITS))]
    q_a, kv_a, k_r, q_s, k_s, v_s, q_i, k_i, w_i = seg
    pad = jnp.zeros((D_MODEL, LANES - IDX_DIM - MLA_ROPE - IDX_HEADS), w_in.dtype)
    w = jnp.concatenate([q_a, kv_a, q_s, k_s, v_s, q_i, k_i, k_r, w_i, pad], 1).astype(BF16)
    wqn = w_q_b[:, :, :MLA_NOPE].reshape(Q_LORA, MLA_HEADS * MLA_NOPE).astype(BF16)
    wqr = w_q_b[:, :, MLA_NOPE:].reshape(Q_LORA, MLA_HEADS * MLA_ROPE).astype(BF16)
    eye = jnp.eye(MLA_HEADS, dtype=F32)
    bdk = jnp.einsum('chn,hg->hngc', w_kv_b[:, :, :MLA_NOPE], eye).reshape(
        MLA_HEADS * MLA_NOPE, MLA_HEADS * KV_LORA).astype(BF16)
    bdv = jnp.einsum('chv,hg->hcgv', w_kv_b[:, :, MLA_NOPE:], eye).reshape(
        MLA_HEADS * KV_LORA, MLA_HEADS * MLA_V).astype(BF16)
    lpad = jnp.zeros((LANES - IDX_DIM,), F32)
    ikg = jnp.concatenate([ik_g, lpad])[None, :]
    ikb = jnp.concatenate([ik_b, lpad])[None, :]
    return (w, wqn, wqr, bdk, q_g[None, :], kv_g[None, :], ikg, ikb), bdv


def _softmax_step(s, v_bf16, m_s, l_s, acc_s):
    m_prev = m_s[...]
    m_new = jnp.maximum(m_prev, jnp.max(s, -1, keepdims=True))
    alpha = jnp.exp(m_prev - m_new)
    p = jnp.exp(s - m_new)
    l_s[...] = alpha * l_s[...] + jnp.sum(p, -1, keepdims=True)
    acc_s[...] = alpha * acc_s[...] + _dot(p.astype(BF16), v_bf16)
    m_s[...] = m_new


def _softmax_init(m_s, l_s, acc_s):
    m_s[...] = jnp.full(m_s.shape, NEG, F32)
    l_s[...] = jnp.zeros(l_s.shape, F32)
    acc_s[...] = jnp.zeros(acc_s.shape, F32)


def _mla_prompt_kernel(q_ref, rows_ref, bdv_ref, o_ref, m_s, l_s, acc_s, *, tq, tk):
    i = pl.program_id(1)
    q = q_ref[...].reshape(MLA_HEADS * tq, MLA_CACHE_W)
    _softmax_init(m_s, l_s, acc_s)

    def step(j, masked):
        kv = rows_ref[pl.ds(pl.multiple_of(j * tk, tk), tk), :].astype(BF16)
        s = _dot_t(q, kv) * MLA_SCALE
        if masked:
            kpos = j * tk + lax.broadcasted_iota(I32, (1, tk), 1)
            qpos = i * tq + lax.broadcasted_iota(I32, (MLA_HEADS, tq, 1), 1).reshape(MLA_HEADS * tq, 1)
            s = jnp.where(kpos <= qpos, s, NEG)
        _softmax_step(s, kv[:, 0:KV_LORA], m_s, l_s, acc_s)

    n_full = (i * tq) // tk

    def body(j, carry):
        step(j, False)
        return carry

    lax.fori_loop(0, n_full, body, 0)
    step(n_full, True)
    o = acc_s[...] / l_s[...]
    o_tok = jnp.concatenate([o[h * tq:(h + 1) * tq] for h in range(MLA_HEADS)], axis=1).astype(BF16)
    o_ref[...] = _dot(o_tok, bdv_ref[...]).astype(BF16)


def _mla_prompt(mla_q, rows, bdv, n_b, t, tq, tk):
    nq = t // tq
    rows8 = MLA_HEADS * tq
    return pl.pallas_call(
        functools.partial(_mla_prompt_kernel, tq=tq, tk=tk),
        grid=(n_b, nq),
        in_specs=[
            pl.BlockSpec((MLA_HEADS, tq, MLA_CACHE_W), lambda b, i: (0, b * nq + i, 0)),
            pl.BlockSpec((t, MLA_CACHE_W), lambda b, i: (b, 0)),
            pl.BlockSpec(bdv.shape, lambda b, i: (0, 0)),
        ],
        out_specs=pl.BlockSpec((tq, MLA_HEADS * MLA_V), lambda b, i: (b * nq + i, 0)),
        out_shape=jax.ShapeDtypeStruct((n_b * t, MLA_HEADS * MLA_V), BF16),
        scratch_shapes=[pltpu.VMEM((rows8, 1), F32), pltpu.VMEM((rows8, 1), F32),
                        pltpu.VMEM((rows8, KV_LORA), F32)],
        compiler_params=_cparams(("parallel", "parallel")),
        name="mla_prompt",
    )(mla_q, rows, bdv)


def _score_keys(score):
    bits = lax.bitcast_convert_type(score, I32)
    bits = jnp.where(bits == INT_MIN, 0, bits)
    return bits ^ ((bits >> 31) & 0x7FFFFFFF)


def _expand_dsa_q(dq, rows):
    lane = lax.broadcasted_iota(I32, (rows, LANES), 1)
    out = []
    for h in range(DSA_HEADS):
        blk = dq[:, LANES * (h // 2):LANES * (h // 2 + 1)].astype(F32)
        grp = h // DSA_GROUP
        src = blk if (h % 2) == grp else _roll(blk, HEAD_DIM)
        keep = (lane < HEAD_DIM) if grp == 0 else (lane >= HEAD_DIM)
        out.append(jnp.where(keep, src, 0.0).astype(BF16))
    return out


def _stack_idx_q(iq):
    out = []
    for h in range(IDX_HEADS):
        blk = iq[:, LANES * (h // 2):LANES * (h // 2 + 1)].astype(F32)
        src = blk if h % 2 == 0 else _roll(blk, HEAD_DIM)
        out.append(src[:, 0:IDX_DIM].astype(BF16))
    return out


def _collapse_dsa_o(o_heads, rows):
    lane = lax.broadcasted_iota(I32, (rows, LANES), 1)
    cols = []
    for j in range(DSA_HEADS // 2):
        grp = (2 * j) // DSA_GROUP
        a, b = o_heads[2 * j], o_heads[2 * j + 1]
        left = a if grp == 0 else _roll(a, HEAD_DIM)
        right = _roll(b, HEAD_DIM) if grp == 0 else b
        cols.append(jnp.where(lane < HEAD_DIM, left, right))
    return jnp.concatenate(cols, axis=1)


def _tie_bias(k, thr, need, run, tri_ref):
    eqf = jnp.where(k == thr, 1.0, 0.0)
    parts = []
    for blk in range(k.shape[1] // MXU_DIM):
        e = eqf[:, MXU_DIM * blk:MXU_DIM * (blk + 1)]
        parts.append(_dot(e.astype(BF16), tri_ref[...]) + run)
        run = run + jnp.sum(e, -1, keepdims=True)
    pre = parts[0] if len(parts) == 1 else jnp.concatenate(parts, axis=1)
    t_eq = jnp.where(pre < need, 0.0, NEG)
    t_eq = jnp.where(k == INT_MIN, NEG, t_eq)
    bias = jnp.where(k > thr, 0.0, jnp.where(k == thr, t_eq, NEG))
    return bias, run


def _dsa_prompt_kernel(iq_ref, iw_ref, dq_ref, ik_ref, dk_ref, dv_ref, tri_ref, o_ref,
                       key_s, m_s, l_s, acc_s, *, tq, ck, n_sel):
    i = pl.program_id(1)
    rows8 = DSA_HEADS * tq
    n_ck = (i * tq + tq + ck - 1) // ck
    qi = jnp.concatenate(_stack_idx_q(iq_ref[...]), axis=0)
    w = iw_ref[...]
    wcol = jnp.concatenate([w[:, h:h + 1] for h in range(IDX_HEADS)], axis=0)
    qpos = i * tq + lax.broadcasted_iota(I32, (tq, 1), 0)

    def score_chunk(c, carry):
        kc = ik_ref[pl.ds(pl.multiple_of(c * ck, ck), ck), :].astype(BF16)
        s = jnp.maximum(_dot_t(qi, kc), 0.0) * wcol
        score = jnp.sum(s.reshape(IDX_HEADS, tq, ck), axis=0)
        kpos = c * ck + lax.broadcasted_iota(I32, (1, ck), 1)
        key_s[c] = jnp.where(kpos <= qpos, _score_keys(score), INT_MIN)
        return carry

    lax.fori_loop(0, n_ck, score_chunk, 0)

    def count(cand, strict):
        def body(c, acc):
            k = key_s[c]
            hit = jnp.where((k > cand) if strict else (k >= cand), 1.0, 0.0)
            for blk in range(ck // LANES):
                acc = acc + hit[:, LANES * blk:LANES * (blk + 1)]
            return acc
        acc = lax.fori_loop(0, n_ck, body, jnp.zeros((tq, LANES), F32))
        return jnp.sum(acc, -1, keepdims=True)

    def bit_step(b, res):
        cand = res | jnp.left_shift(jnp.int32(1), 31 - b)
        return jnp.where(count(cand ^ INT_MIN, False) >= n_sel, cand, res)

    thr = lax.fori_loop(0, 32, bit_step, jnp.zeros((tq, 1), I32)) ^ INT_MIN
    need = n_sel - count(thr, True)

    q = jnp.concatenate(_expand_dsa_q(dq_ref[...], tq), axis=0)
    _softmax_init(m_s, l_s, acc_s)

    def attend_chunk(c, run):
        bias, run = _tie_bias(key_s[c], thr, need, run, tri_ref)
        start = pl.multiple_of(c * ck, ck)
        kb = dk_ref[pl.ds(start, ck), :].astype(BF16)
        vb = dv_ref[pl.ds(start, ck), :].astype(BF16)
        s = (_dot_t(q, kb).reshape(DSA_HEADS, tq, ck) + bias[None]).reshape(rows8, ck)
        _softmax_step(s, vb, m_s, l_s, acc_s)
        return run

    lax.fori_loop(0, n_ck, attend_chunk, jnp.zeros((tq, 1), F32))
    o = acc_s[...] / l_s[...]
    o_ref[...] = _collapse_dsa_o([o[h * tq:(h + 1) * tq] for h in range(DSA_HEADS)], tq).astype(BF16)


def _dsa_prompt(iq, iw, dq, ik, dk, dv, tri, n_b, t, tq, ck):
    nq = t // tq
    n_sel = min(TOPK_MAX, t // 4)
    rows8 = DSA_HEADS * tq
    qrow = lambda b, i: (b * nq + i, 0)
    brow = lambda b, i: (b, 0)
    return pl.pallas_call(
        functools.partial(_dsa_prompt_kernel, tq=tq, ck=ck, n_sel=n_sel),
        grid=(n_b, nq),
        in_specs=[
            pl.BlockSpec((tq, IDX_HEADS * IDX_DIM), qrow),
            pl.BlockSpec((tq, IDX_HEADS), qrow),
            pl.BlockSpec((tq, DSA_HEADS * HEAD_DIM), qrow),
            pl.BlockSpec((t, IDX_DIM), brow),
            pl.BlockSpec((t, DSA_KV_HEADS * HEAD_DIM), brow),
            pl.BlockSpec((t, DSA_KV_HEADS * HEAD_DIM), brow),
            pl.BlockSpec(tri.shape, lambda b, i: (0, 0)),
        ],
        out_specs=pl.BlockSpec((tq, DSA_HEADS * HEAD_DIM), qrow),
        out_shape=jax.ShapeDtypeStruct((n_b * t, DSA_HEADS * HEAD_DIM), BF16),
        scratch_shapes=[pltpu.VMEM((t // ck, tq, ck), I32),
                        pltpu.VMEM((rows8, 1), F32), pltpu.VMEM((rows8, 1), F32),
                        pltpu.VMEM((rows8, LANES), F32)],
        compiler_params=_cparams(("parallel", "parallel")),
        name="dsa_prompt",
    )(iq, iw, dq, ik, dk, dv, tri)


def _page_copies(pool_ref, layer, pt_ref, b, buf_ref, sem_ref, slot, n_pages):
    return [pltpu.make_async_copy(pool_ref.at[layer, pt_ref[b, p]],
                                  buf_ref.at[slot, :, pl.ds(p * PAGE_SIZE, PAGE_SIZE)], sem_ref.at[slot])
            for p in range(n_pages)]


def _fetch_pages(copies, b, n_b, slot):
    @pl.when(b == 0)
    def _():
        for cp in copies(b, slot):
            cp.start()

    @pl.when(b + 1 < n_b)
    def _():
        for cp in copies(b + 1, 1 - slot):
            cp.start()

    for cp in copies(b, slot):
        cp.wait()


def _mla_sample_kernel(pt_ref, q_ref, new_ref, pool_ref, o_ref, buf, sem, *, layer, n_pages):
    b = pl.program_id(0)
    slot = b % 2
    _fetch_pages(lambda bb, sl: _page_copies(pool_ref, layer, pt_ref, bb, buf, sem, sl, n_pages),
                 b, pl.num_programs(0), slot)

    q = q_ref[0]
    kt = buf[slot].astype(BF16)
    s = _dot(q, kt) * MLA_SCALE
    row = new_ref[0].astype(BF16).astype(F32)
    s_new = jnp.sum(q.astype(F32) * row, -1, keepdims=True) * MLA_SCALE
    m = jnp.maximum(jnp.max(s, -1, keepdims=True), s_new)
    p = jnp.exp(s - m)
    p_new = jnp.exp(s_new - m)
    l = jnp.sum(p, -1, keepdims=True) + p_new
    acc = _dot_t(p.astype(BF16), kt[0:KV_LORA]) + p_new.astype(BF16).astype(F32) * row[:, 0:KV_LORA]
    o_ref[0] = acc / l


def _mla_sample(page_table, q, new_rows, pool, layer):
    n_b, n_pages = page_table.shape
    return pl.pallas_call(
        functools.partial(_mla_sample_kernel, layer=layer, n_pages=n_pages),
        grid_spec=pltpu.PrefetchScalarGridSpec(
            num_scalar_prefetch=1,
            grid=(n_b,),
            in_specs=[
                pl.BlockSpec((1, MLA_HEADS, MLA_CACHE_W), lambda b, pt: (b, 0, 0)),
                pl.BlockSpec((1, 1, MLA_CACHE_W), lambda b, pt: (b, 0, 0)),
                pl.BlockSpec(memory_space=pl.ANY),
            ],
            out_specs=pl.BlockSpec((1, MLA_HEADS, KV_LORA), lambda b, pt: (b, 0, 0)),
            scratch_shapes=[pltpu.VMEM((2, MLA_CACHE_W, n_pages * PAGE_SIZE), F32),
                            pltpu.SemaphoreType.DMA((2,))],
        ),
        out_shape=jax.ShapeDtypeStruct((n_b, MLA_HEADS, KV_LORA), F32),
        compiler_params=_cparams(("arbitrary",)),
        name="mla_sample",
    )(page_table, q, new_rows, pool)


def _dsa_sample_kernel(pt_ref, iq_ref, iw_ref, dq_ref, ikn_ref, dkn_ref, dvn_ref, tri_ref,
                       pool_ik, pool_k, pool_v, o_ref,
                       buf_ik, buf_k, buf_v, sem, key_s, bias_s,
                       *, layer, n_pages, n_sel):
    b = pl.program_id(0)
    slot = b % 2

    def copies(bb, sl):
        return (_page_copies(pool_ik, layer, pt_ref, bb, buf_ik, sem.at[0], sl, n_pages)
                + _page_copies(pool_k, layer, pt_ref, bb, buf_k, sem.at[1], sl, n_pages)
                + _page_copies(pool_v, layer, pt_ref, bb, buf_v, sem.at[2], sl, n_pages))

    _fetch_pages(copies, b, pl.num_programs(0), slot)

    n_h = IDX_HEADS
    n_keys = n_pages * PAGE_SIZE
    qi = iq_ref[0]
    wcol = iw_ref[0]

    s = jnp.maximum(_dot(qi, buf_ik[slot].astype(BF16)), 0.0) * wcol
    key_s[...] = _score_keys(jnp.broadcast_to(jnp.sum(s, axis=0, keepdims=True), (n_h, n_keys)))
    kn = ikn_ref[0].astype(BF16).astype(F32)
    s_n = jnp.maximum(jnp.sum(qi.astype(F32) * kn, -1, keepdims=True), 0.0) * wcol
    s_n = jnp.broadcast_to(jnp.sum(jnp.broadcast_to(s_n, (n_h, LANES)), axis=0, keepdims=True), (n_h, LANES))
    key_new = _score_keys(s_n)[:, 0:1]

    def count(cand, strict):
        k = key_s[...]
        hit = jnp.where((k > cand) if strict else (k >= cand), 1.0, 0.0)
        hit_n = jnp.where((key_new > cand) if strict else (key_new >= cand), 1.0, 0.0)
        return jnp.sum(hit, -1, keepdims=True) + hit_n

    def bit_step(bit, res):
        cand = res | jnp.left_shift(jnp.int32(1), 31 - bit)
        return jnp.where(count(cand ^ INT_MIN, False) >= n_sel, cand, res)

    thr = lax.fori_loop(0, 32, bit_step, jnp.zeros((n_h, 1), I32)) ^ INT_MIN
    need = n_sel - count(thr, True)
    n_tie = count(thr, False) - (n_sel - need)
    new_tie = jnp.where(key_new == thr, 1.0, 0.0)

    def tie_break():
        bias, run = _tie_bias(key_s[...], thr, need, jnp.zeros((n_h, 1), F32), tri_ref)
        bias_s[...] = bias
        return run

    def admit_all():
        bias_s[...] = jnp.where(key_s[...] >= thr, 0.0, NEG)
        return n_tie - new_tie

    run = lax.cond(jnp.max(n_tie - need) > 0.0, tie_break, admit_all)
    bias_new = jnp.where(key_new > thr, 0.0, jnp.where(key_new == thr, jnp.where(run < need, 0.0, NEG), NEG))

    q = dq_ref[0]
    s = _dot(q, buf_k[slot].astype(BF16)) + bias_s[...]
    k_n = dkn_ref[0].astype(BF16).astype(F32)
    v_n = dvn_ref[0].astype(BF16).astype(F32)
    s_new = jnp.sum(q.astype(F32) * k_n, -1, keepdims=True) + bias_new
    m = jnp.maximum(jnp.max(s, -1, keepdims=True), s_new)
    p = jnp.exp(s - m)
    p_new = jnp.exp(s_new - m)
    l = jnp.sum(p, -1, keepdims=True) + p_new
    acc = _dot_t(p.astype(BF16), buf_v[slot].astype(BF16)) + p_new.astype(BF16).astype(F32) * v_n
    o = acc / l
    head = lax.broadcasted_iota(I32, (n_h, LANES), 0)
    o = jnp.where(head < DSA_GROUP, o, _roll(o, HEAD_DIM))
    o_ref[0] = o[:, 0:HEAD_DIM]


def _dsa_sample(page_table, iq, iw, dq, ik_new, dk_new, dv_new, tri, pool_ik, pool_k, pool_v, layer):
    n_b, n_pages = page_table.shape
    n_keys = n_pages * PAGE_SIZE
    n_sel = min(TOPK_MAX, (n_keys + 1) // 4)
    per_b = lambda b, pt: (b, 0, 0)
    any_spec = pl.BlockSpec(memory_space=pl.ANY)
    kvw = DSA_KV_HEADS * HEAD_DIM
    return pl.pallas_call(
        functools.partial(_dsa_sample_kernel, layer=layer, n_pages=n_pages, n_sel=n_sel),
        grid_spec=pltpu.PrefetchScalarGridSpec(
            num_scalar_prefetch=1,
            grid=(n_b,),
            in_specs=[
                pl.BlockSpec((1, IDX_HEADS, IDX_DIM), per_b),
                pl.BlockSpec((1, IDX_HEADS, 1), per_b),
                pl.BlockSpec((1, DSA_HEADS, LANES), per_b),
                pl.BlockSpec((1, 1, IDX_DIM), per_b),
                pl.BlockSpec((1, 1, kvw), per_b),
                pl.BlockSpec((1, 1, kvw), per_b),
                pl.BlockSpec(tri.shape, lambda b, pt: (0, 0)),
                any_spec, any_spec, any_spec,
            ],
            out_specs=pl.BlockSpec((1, DSA_HEADS, HEAD_DIM), per_b),
            scratch_shapes=[pltpu.VMEM((2, IDX_DIM, n_keys), F32),
                            pltpu.VMEM((2, kvw, n_keys), F32),
                            pltpu.VMEM((2, kvw, n_keys), F32),
                            pltpu.SemaphoreType.DMA((3, 2)),
                            pltpu.VMEM((IDX_HEADS, n_keys), I32),
                            pltpu.VMEM((IDX_HEADS, n_keys), F32)],
        ),
        out_shape=jax.ShapeDtypeStruct((n_b, DSA_HEADS, HEAD_DIM), F32),
        compiler_params=_cparams(("arbitrary",)),
        name="dsa_sample",
    )(page_table, iq, iw, dq, ik_new, dk_new, dv_new, tri, pool_ik, pool_k, pool_v)


def _proj_c_kernel(x_ref, tab_ref, w_ref, q_ref, k_ref, v_ref):
    z = _dot(x_ref[...].astype(BF16), w_ref[...])
    ch, s1h, s2h = tab_ref[0], tab_ref[1], tab_ref[2]
    width = DIL_HEADS * HEAD_DIM
    for c in range(width // LANES):
        v = z[:, LANES * c:LANES * (c + 1)]
        q_ref[:, LANES * c:LANES * (c + 1)] = (
            v * ch + _roll(v, ROT_DIM // 2) * s1h + _roll(v, LANES - ROT_DIM // 2) * s2h) * ATTN_SCALE
        v = z[:, width + LANES * c:width + LANES * (c + 1)]
        k_ref[:, LANES * c:LANES * (c + 1)] = (
            v * ch + _roll(v, ROT_DIM // 2) * s1h + _roll(v, LANES - ROT_DIM // 2) * s2h)
    v_ref[...] = z[:, 2 * width:3 * width]


def _proj_c(x, tabs, w, tm):
    m = x.shape[0]
    width = DIL_HEADS * HEAD_DIM
    row = lambda i: (i, 0)
    sds = jax.ShapeDtypeStruct((m, width), F32)
    return pl.pallas_call(
        _proj_c_kernel,
        grid=(m // tm,),
        in_specs=[pl.BlockSpec((tm, D_MODEL), row),
                  pl.BlockSpec((3, tm, LANES), lambda i: (0, i, 0)),
                  pl.BlockSpec(w.shape, lambda i: (0, 0))],
        out_specs=(pl.BlockSpec((tm, width), row),) * 3,
        out_shape=(sds, sds, sds),
        compiler_params=_cparams(("parallel",)),
        name="proj_c",
    )(x, tabs, w)


def _dil_prompt_kernel(q_ref, k_ref, v_ref, o_ref, og_s, lse_s, *, t, tu, mc):
    lane = lax.broadcasted_iota(I32, (tu, LANES), 1)
    low = lane < HEAD_DIM
    row2 = lax.broadcasted_iota(I32, (2, tu, 1), 1).reshape(2 * tu, 1)

    for g, (window, dil) in enumerate(DIL_PATTERNS):
        n_ub = t // dil // tu
        back = window // dil

        def unit(u, carry, g=g, dil=dil, n_ub=n_ub, back=back):
            r = u // n_ub
            ub = u % n_ub
            q_start = r + dil * tu * ub
            qf = q_ref[pl.ds(q_start, tu, stride=dil), :]
            q2 = jnp.concatenate([jnp.where(low, qf, 0.0), jnp.where(low, 0.0, qf)], axis=0).astype(BF16)
            u0 = jnp.maximum(ub * tu - back, 0)
            k_start = r + dil * u0
            kb = k_ref[pl.ds(k_start, 2 * tu, stride=dil), :].astype(BF16)
            vb = v_ref[pl.ds(k_start, 2 * tu, stride=dil), :].astype(BF16)
            s = _dot_t(q2, kb)
            uq = ub * tu + row2
            uk = u0 + lax.broadcasted_iota(I32, (1, 2 * tu), 1)
            s = jnp.where((uk <= uq) & (uk >= uq - back), s, NEG)
            m = jnp.max(s, -1, keepdims=True)
            p = jnp.exp(s - m)
            l = jnp.sum(p, -1, keepdims=True)
            o = _dot(p.astype(BF16), vb) / l
            lse = m + jnp.log(l)
            og_s[g, pl.ds(q_start, tu, stride=dil), :] = jnp.where(low, o[0:tu], o[tu:2 * tu])
            lse_s[g, pl.ds(q_start, tu, stride=dil), :] = jnp.where(
                low, jnp.broadcast_to(lse[0:tu], (tu, LANES)), jnp.broadcast_to(lse[tu:2 * tu], (tu, LANES)))
            return carry

        lax.fori_loop(0, dil * n_ub, unit, 0)

    def merge(c, carry):
        sl = pl.ds(pl.multiple_of(c * mc, mc), mc)
        l0, l1, l2 = lse_s[0, sl, :], lse_s[1, sl, :], lse_s[2, sl, :]
        mx = jnp.maximum(jnp.maximum(l0, l1), l2)
        w0, w1, w2 = jnp.exp(l0 - mx), jnp.exp(l1 - mx), jnp.exp(l2 - mx)
        o = (w0 * og_s[0, sl, :] + w1 * og_s[1, sl, :] + w2 * og_s[2, sl, :]) / (w0 + w1 + w2)
        o_ref[sl, :] = o.astype(BF16)
        return carry

    lax.fori_loop(0, t // mc, merge, 0)


def _dil_prompt(q, k, v, n_b, t, tu=128, mc=512):
    n_pair = DIL_HEADS * HEAD_DIM // LANES
    spec = pl.BlockSpec((t, LANES), lambda b, p: (b, p))
    return pl.pallas_call(
        functools.partial(_dil_prompt_kernel, t=t, tu=tu, mc=mc),
        grid=(n_b, n_pair),
        in_specs=[spec, spec, spec],
        out_specs=spec,
        out_shape=jax.ShapeDtypeStruct((n_b * t, DIL_HEADS * HEAD_DIM), BF16),
        scratch_shapes=[pltpu.VMEM((len(DIL_PATTERNS), t, LANES), F32),
                        pltpu.VMEM((len(DIL_PATTERNS), t, LANES), F32)],
        compiler_params=_cparams(("parallel", "parallel")),
        name="dil_prompt",
    )(q, k, v)


def _dil_sample_kernel(*refs, fb, has_prev):
    q_ref, kr_ref, vr_ref, kc_ref, vc_ref, sk_ref, sv_ref = refs[:7]
    o_ref, ko_ref, vo_ref = refs[-3:]
    del has_prev
    w = WIN_MAX
    q = q_ref[0]
    rows = 8
    head_of_lane = lax.broadcasted_iota(I32, (rows, fb), 1) // HEAD_DIM
    own = head_of_lane == lax.broadcasted_iota(I32, (rows, fb), 0)
    qe16 = jnp.where(own, jnp.broadcast_to(q, (rows, fb)), 0.0).astype(BF16)
    kt = sk_ref[0, 0]
    vt = sv_ref[0, 0]
    s = _dot(qe16, kt.astype(BF16))
    t = lax.broadcasted_iota(I32, (1, w), 1)
    cnt = jnp.zeros((1, w), F32)
    for window, dil in DIL_PATTERNS:
        cnt = cnt + jnp.where((t >= w - window) & ((t & (dil - 1)) == 0), 1.0, 0.0)
    s = jnp.where(cnt > 0.0, s, NEG)
    k_n = kr_ref[0].astype(BF16).astype(F32)
    v_n = vr_ref[0].astype(BF16).astype(F32)
    s_new = jnp.sum(qe16.astype(F32) * k_n, -1, keepdims=True)
    m = jnp.maximum(jnp.max(s, -1, keepdims=True), s_new)
    p = jnp.exp(s - m) * cnt
    p_new = jnp.exp(s_new - m)
    n_pat = float(len(DIL_PATTERNS))
    l = jnp.sum(p, -1, keepdims=True) + n_pat * p_new
    acc = _dot_t(p.astype(BF16), vt.astype(BF16)) + n_pat * p_new.astype(BF16).astype(F32) * v_n
    o_ref[0] = jnp.sum(jnp.where(own, acc / l, 0.0), axis=0, keepdims=True)
    last = lax.broadcasted_iota(I32, (fb, w), 1) == w - 1
    ko_ref[0, 0] = jnp.where(last, kc_ref[0], _roll(kt, w - 1))
    vo_ref[0, 0] = jnp.where(last, vc_ref[0], _roll(vt, w - 1))


def _dil_sample(q, k_new, v_new, state_k, state_v, layer, prev=None, fb=256):
    n_b, width, w = state_k.shape[1:]
    row = pl.BlockSpec((1, 1, fb), lambda b, f: (b, 0, f))
    col = pl.BlockSpec((1, fb, 1), lambda b, f: (b, f, 0))
    st = pl.BlockSpec((1, 1, fb, w), lambda b, f: (layer, b, f, 0))
    ins = [q[:, None, :], k_new[:, None, :], v_new[:, None, :], k_new[:, :, None], v_new[:, :, None],
           state_k, state_v]
    in_specs = [row, row, row, col, col, st, st]
    aliases = {}
    if prev is not None:
        ins += list(prev)
        in_specs += [pl.BlockSpec(memory_space=pl.ANY)] * 2
        aliases = {7: 1, 8: 2}
    sds = jax.ShapeDtypeStruct(state_k.shape, F32)
    return pl.pallas_call(
        functools.partial(_dil_sample_kernel, fb=fb, has_prev=prev is not None),
        grid=(n_b, width // fb),
        in_specs=in_specs,
        out_specs=(row, st, st),
        out_shape=(jax.ShapeDtypeStruct((n_b, 1, width), F32), sds, sds),
        input_output_aliases=aliases,
        compiler_params=_cparams(("parallel", "parallel")),
        name="dil_sample",
    )(*ins)


def _mm_kernel(a_ref, w_ref, o_ref):
    o_ref[...] = _dot(a_ref[...].astype(BF16), w_ref[...]).astype(o_ref.dtype)


def _mm(a, w, out_dtype, tm):
    m, k = a.shape
    n = w.shape[1]
    return pl.pallas_call(
        _mm_kernel,
        grid=(m // tm,),
        in_specs=[pl.BlockSpec((tm, k), lambda i: (i, 0)), pl.BlockSpec((k, n), lambda i: (0, 0))],
        out_specs=pl.BlockSpec((tm, n), lambda i: (i, 0)),
        out_shape=jax.ShapeDtypeStruct((m, n), out_dtype),
        compiler_params=_cparams(("parallel",)),
        name="mm",
    )(a, w)


def _mm_ln_kernel(*refs, n_in):
    x_ref = refs[0]
    a_refs = refs[1:1 + n_in]
    w_refs = refs[1 + n_in:1 + 2 * n_in]
    g_ref, b_ref, o_ref = refs[1 + 2 * n_in:]
    acc = DEEPNORM_ALPHA * x_ref[...]
    for a_ref, w_ref in zip(a_refs, w_refs):
        acc = acc + _dot(a_ref[...].astype(BF16), w_ref[...])
    mu = jnp.mean(acc, -1, keepdims=True)
    dlt = acc - mu
    var = jnp.mean(dlt * dlt, -1, keepdims=True)
    o_ref[...] = dlt * lax.rsqrt(var + 1e-5) * g_ref[...] + b_ref[...]


def _mm_ln(x, a_list, w_list, g, b, tm):
    m = x.shape[0]
    n_in = len(a_list)
    row = lambda i: (i, 0)
    fix = lambda i: (0, 0)
    in_specs = ([pl.BlockSpec((tm, D_MODEL), row)]
                + [pl.BlockSpec((tm, a.shape[1]), row) for a in a_list]
                + [pl.BlockSpec(w.shape, fix) for w in w_list]
                + [pl.BlockSpec((1, D_MODEL), fix), pl.BlockSpec((1, D_MODEL), fix)])
    return pl.pallas_call(
        functools.partial(_mm_ln_kernel, n_in=n_in),
        grid=(m // tm,),
        in_specs=in_specs,
        out_specs=pl.BlockSpec((tm, D_MODEL), row),
        out_shape=jax.ShapeDtypeStruct((m, D_MODEL), F32),
        compiler_params=_cparams(("parallel",)),
        name="mm_ln",
    )(x, *a_list, *w_list, g[None, :], b[None, :])


def _swiglu_up_kernel(x_ref, wg_ref, wu_ref, o_ref):
    x = x_ref[...].astype(BF16)
    gate = _dot(x, wg_ref[...])
    up = _dot(x, wu_ref[...])
    o_ref[...] = (gate * jax.nn.sigmoid(gate) * up).astype(BF16)


def _swiglu_up(x, wg, wu, tm):
    m = x.shape[0]
    d_ff = wg.shape[1]
    return pl.pallas_call(
        _swiglu_up_kernel,
        grid=(m // tm,),
        in_specs=[pl.BlockSpec((tm, D_MODEL), lambda i: (i, 0)),
                  pl.BlockSpec(wg.shape, lambda i: (0, 0)), pl.BlockSpec(wu.shape, lambda i: (0, 0))],
        out_specs=pl.BlockSpec((tm, d_ff), lambda i: (i, 0)),
        out_shape=jax.ShapeDtypeStruct((m, d_ff), BF16),
        compiler_params=_cparams(("parallel",)),
        name="swiglu_up",
    )(x, wg, wu)


def _row_tile(m):
    for tm in (256, 128):
        if m % tm == 0:
            return tm
    return m


def kernel(x_prompt, x_sample, cache_mla, cache_dsa_k, cache_dsa_v, cache_idx_k, state_win_k, state_win_v,
           page_table, w_in_ab, mla_q_g, w_mla_q_b, mla_kv_g, w_mla_kv_b, idx_k_g, idx_k_b, w_o_ab,
           w_in_c, w_o_c, ln_g, ln_b, w_gate, w_up, w_down):
    n_b, t = x_prompt.shape[:2]
    n_dec, n_new = x_sample.shape[:2]
    n_pages = page_table.shape[1]
    past = n_pages * PAGE_SIZE
    win_buf = state_win_k.shape[2]
    assert n_new == 1 and win_buf == WIN_MAX and t % (16 * 256) == 0 and n_pages % 8 == 0
    mp, ms = n_b * t, n_dec
    tm_p, tm_s = _row_tile(mp), _row_tile(ms)

    pos_p = jnp.tile(jnp.arange(t, dtype=I32), n_b)
    pos_s = jnp.full((ms,), past, I32)
    tab_ab_p, tab_ab_s = _ab_tables(pos_p), _ab_tables(pos_s)
    tab_h_p, tab_h_s = _head_tables(pos_p), _head_tables(pos_s)
    tri = jnp.triu(jnp.ones((MXU_DIM, MXU_DIM), F32), 1).astype(BF16)
    n_l, n_pool = cache_mla.shape[:2]
    pool_kw = DSA_KV_HEADS * HEAD_DIM
    pool_mla = jnp.transpose(cache_mla, (0, 1, 3, 2))
    pool_ik = jnp.transpose(cache_idx_k, (0, 1, 3, 2))
    pool_k = jnp.transpose(cache_dsa_k, (0, 1, 3, 4, 2)).reshape(n_l, n_pool, pool_kw, PAGE_SIZE)
    pool_v = jnp.transpose(cache_dsa_v, (0, 1, 3, 4, 2)).reshape(n_l, n_pool, pool_kw, PAGE_SIZE)
    width_c = DIL_HEADS * HEAD_DIM
    win_k = jnp.transpose(state_win_k, (0, 1, 3, 4, 2)).reshape(state_win_k.shape[0], n_dec, width_c, win_buf)
    win_v = jnp.transpose(state_win_v, (0, 1, 3, 4, 2)).reshape(state_win_v.shape[0], n_dec, width_c, win_buf)
    new_win = None

    hp = x_prompt.reshape(mp, D_MODEL)
    hs = x_sample.reshape(ms, D_MODEL)
    outs = {k: [] for k in ("mla_p", "mla_s", "dk_p", "dk_s", "dv_p", "dv_s", "ik_p", "ik_s",
                            "wk_p", "wk_s", "wv_p", "wv_s")}
    for layer in range(DEPTH):
        if layer % 2 == 0:
            i = layer // 2
            wts, bdv = _ab_weights(w_in_ab[i], mla_q_g[i], w_mla_q_b[i], mla_kv_g[i], w_mla_kv_b[i],
                                   idx_k_g[i], idx_k_b[i])
            w_o = w_o_ab[i].astype(BF16)
            w_o_mla, w_o_dsa = w_o[:MLA_HEADS * MLA_V], w_o[MLA_HEADS * MLA_V:]
            mla_q, rows, dq, dk, dv, iq, ik, iw = _proj_ab(hp, tab_ab_p, wts, tm_p)
            o_mla = _mla_prompt(mla_q, rows, bdv, n_b, t, tq=128, tk=512)
            o_dsa = _dsa_prompt(iq, iw, dq, ik, dk, dv, tri, n_b, t, tq=128, ck=512)
            mix_in_p = ([o_mla, o_dsa], [w_o_mla, w_o_dsa])
            outs["mla_p"].append(rows.reshape(n_b, t, MLA_CACHE_W))
            outs["dk_p"].append(dk.reshape(n_b, t, DSA_KV_HEADS, HEAD_DIM))
            outs["dv_p"].append(dv.reshape(n_b, t, DSA_KV_HEADS, HEAD_DIM))
            outs["ik_p"].append(ik.reshape(n_b, t, IDX_DIM))
            mla_q, rows, dq, dk, dv, iq, ik, iw = _proj_ab(hs, tab_ab_s, wts, tm_s)
            o_lat = _mla_sample(page_table, jnp.transpose(mla_q, (1, 0, 2)), rows[:, None, :], pool_mla, i)
            o_mla = _mm(o_lat.reshape(ms, MLA_HEADS * KV_LORA), bdv, BF16, tm_s)
            iq_h = iq.reshape(ms, IDX_HEADS, IDX_DIM)
            dq_h = dq.reshape(ms, DSA_KV_HEADS, DSA_GROUP, HEAD_DIM)
            zero = jnp.zeros_like(dq_h[:, 0])
            dq_e = jnp.concatenate([jnp.concatenate([dq_h[:, 0], zero], -1),
                                    jnp.concatenate([zero, dq_h[:, 1]], -1)], 1)
            o_e = _dsa_sample(page_table, iq_h, iw[:, :, None], dq_e, ik[:, None, :], dk[:, None, :],
                              dv[:, None, :], tri, pool_ik, pool_k, pool_v, i)
            mix_in_s = ([o_mla, o_e.reshape(ms, DSA_HEADS * HEAD_DIM)], [w_o_mla, w_o_dsa])
            outs["mla_s"].append(rows.reshape(n_dec, n_new, MLA_CACHE_W))
            outs["dk_s"].append(dk.reshape(n_dec, n_new, DSA_KV_HEADS, HEAD_DIM))
            outs["dv_s"].append(dv.reshape(n_dec, n_new, DSA_KV_HEADS, HEAD_DIM))
            outs["ik_s"].append(ik.reshape(n_dec, n_new, IDX_DIM))
        else:
            j = layer // 2
            w_c = w_in_c[j].astype(BF16)
            w_o = w_o_c[j].astype(BF16)
            q, k, v = _proj_c(hp, tab_h_p, w_c, tm_p)
            o = _dil_prompt(q, k, v, n_b, t)
            mix_in_p = ([o], [w_o])
            keep = min(WIN_MAX, t)
            outs["wk_p"].append(k.reshape(n_b, t, DIL_HEADS, HEAD_DIM)[:, t - keep:])
            outs["wv_p"].append(v.reshape(n_b, t, DIL_HEADS, HEAD_DIM)[:, t - keep:])
            q, k, v = _proj_c(hs, tab_h_s, w_c, tm_s)
            o, new_k, new_v = _dil_sample(q, k, v, win_k, win_v, j, prev=new_win)
            new_win = (new_k, new_v)
            mix_in_s = ([o.reshape(ms, DIL_HEADS * HEAD_DIM)], [w_o])
        wg, wu, wd = w_gate[layer].astype(BF16), w_up[layer].astype(BF16), w_down[layer].astype(BF16)
        hp = _mm_ln(hp, *mix_in_p, ln_g[layer, 0], ln_b[layer, 0], tm_p)
        hp = _mm_ln(hp, [_swiglu_up(hp, wg, wu, tm_p)], [wd], ln_g[layer, 1], ln_b[layer, 1], tm_p)
        hs = _mm_ln(hs, *mix_in_s, ln_g[layer, 0], ln_b[layer, 0], tm_s)
        hs = _mm_ln(hs, [_swiglu_up(hs, wg, wu, tm_s)], [wd], ln_g[layer, 1], ln_b[layer, 1], tm_s)
    st = {k: jnp.stack(v) for k, v in outs.items() if v}
    wk_s, wv_s = (jnp.transpose(a.reshape(a.shape[0], n_dec, DIL_HEADS, HEAD_DIM, win_buf), (0, 1, 4, 2, 3))
                  for a in new_win)
    return (hp.reshape(n_b, t, D_MODEL), hs.reshape(n_dec, n_new, D_MODEL),
            st["mla_p"], st["mla_s"], st["dk_p"], st["dk_s"], st["dv_p"], st["dv_s"],
            st["ik_p"], st["ik_s"], st["wk_p"], wk_s, st["wv_p"], wv_s)
```

```python
import functools

import numpy as np
import jax
import jax.numpy as jnp
from jax import lax
from jax.experimental import pallas as pl
from jax.experimental.pallas import tpu as pltpu

F32 = jnp.float32
BF16 = jnp.bfloat16
I32 = jnp.int32

D_MODEL = 1024
DEPTH = 4
PAGE_SIZE = 128
HEAD_DIM = 64
ROT_DIM = HEAD_DIM // 4
ROPE_THETA = 500000.0
MLA_HEADS = 8
MLA_NOPE = 64
MLA_ROPE = 32
MLA_V = 64
Q_LORA = 256
KV_LORA = 128
MLA_CACHE_W = KV_LORA + MLA_ROPE
MLA_SCALE = (MLA_NOPE + MLA_ROPE) ** -0.5
DSA_HEADS = 8
DSA_KV_HEADS = 2
DSA_GROUP = DSA_HEADS // DSA_KV_HEADS
IDX_HEADS = 8
IDX_DIM = 64
IDX_ROT = IDX_DIM // 4
IDX_W_SCALE = IDX_HEADS ** -0.5 * IDX_DIM ** -0.5
TOPK_MAX = 256
ATTN_SCALE = HEAD_DIM ** -0.5
DIL_HEADS = 16
DIL_PATTERNS = ((128, 1), (512, 4), (2048, 16))
DIL_KEYS = 128
WIN_MAX = 2048
AB_SPLITS = (Q_LORA, KV_LORA, MLA_ROPE,
             DSA_HEADS * HEAD_DIM, DSA_KV_HEADS * HEAD_DIM, DSA_KV_HEADS * HEAD_DIM,
             IDX_HEADS * IDX_DIM, IDX_DIM, IDX_HEADS)
DEEPNORM_ALPHA = (2 * DEPTH) ** 0.25

LANES = 128
MXU_DIM = 256
VMEM_LIMIT = 52 * 1024 * 1024
NEG = -1e30
INT_MIN = -2 ** 31

_AB_QA, _AB_KVA, _AB_QS, _AB_KS, _AB_VS, _AB_QI, _AB_LAST, _AB_W = 0, 256, 384, 896, 1024, 1152, 1664, 1792
_L_KI, _L_KR, _L_WI = 0, 64, 96


def _cparams(sem):
    return pltpu.CompilerParams(dimension_semantics=sem, vmem_limit_bytes=VMEM_LIMIT)


def _dot(a, b):
    return jnp.dot(a, b, preferred_element_type=F32)


def _dot_t(a, b):
    return lax.dot_general(a, b, (((1,), (1,)), ((), ())), preferred_element_type=F32)


def _roll(x, shift):
    return pltpu.roll(x, shift, 1)


def _rot_tables(pos, offset, span, period, rot_dim):
    half = rot_dim // 2
    inv_freq = ROPE_THETA ** (-(jnp.arange(half, dtype=F32) * 2.0 / rot_dim))
    ang = pos.astype(F32)[:, None] * inv_freq[None, :]
    cos, sin = jnp.cos(ang), jnp.sin(ang)
    m = pos.shape[0]
    one = lambda n: jnp.ones((m, n), F32)
    zero = lambda n: jnp.zeros((m, n), F32)
    rest, tail = period - rot_dim, LANES - offset - span
    reps = span // period
    c = jnp.concatenate([one(offset)] + [cos, cos, one(rest)] * reps + [one(tail)], 1)
    s1 = jnp.concatenate([zero(offset)] + [zero(half), sin, zero(rest)] * reps + [zero(tail)], 1)
    s2 = jnp.concatenate([zero(offset)] + [-sin, zero(half), zero(rest)] * reps + [zero(tail)], 1)
    return c, s1, s2


def _ab_tables(pos):
    ch, s1h, s2h = _rot_tables(pos, 0, LANES, HEAD_DIM, ROT_DIM)
    cr, s1r, s2r = _rot_tables(pos, 0, LANES, MLA_ROPE, MLA_ROPE)
    ci, s1i, s2i = _rot_tables(pos, _L_KI, IDX_DIM, IDX_DIM, IDX_ROT)
    ck, s1k, s2k = _rot_tables(pos, _L_KR, MLA_ROPE, MLA_ROPE, MLA_ROPE)
    return jnp.stack([ch, s1h, s2h, cr, s1r, s2r, ci * ck, s1i, s2i, s1k, s2k])


def _head_tables(pos):
    return jnp.stack(_rot_tables(pos, 0, LANES, HEAD_DIM, ROT_DIM))


def _proj_ab_kernel(x_ref, tab_ref, w_ref, wqn_ref, wqr_ref, bdk_ref, qg_ref, kvg_ref, ikg_ref, ikb_ref,
                    mlaq_ref, rows_ref, dq_ref, dk_ref, dv_ref, iq_ref, ik_ref, iw_ref):
    z = _dot(x_ref[...].astype(BF16), w_ref[...])
    ch, s1h, s2h = tab_ref[0], tab_ref[1], tab_ref[2]

    def rot_h(v):
        return v * ch + _roll(v, ROT_DIM // 2) * s1h + _roll(v, LANES - ROT_DIM // 2) * s2h

    qa = z[:, _AB_QA:_AB_QA + Q_LORA]
    qn = (qa * lax.rsqrt(jnp.mean(qa * qa, -1, keepdims=True) + 1e-6) * qg_ref[...]).astype(BF16)
    q_nope = _dot(qn, wqn_ref[...])
    q_rope = _dot(qn, wqr_ref[...])
    q_lat = _dot(q_nope.astype(BF16), bdk_ref[...])
    cr, s1r, s2r = tab_ref[3], tab_ref[4], tab_ref[5]
    per_blk = LANES // MLA_ROPE
    for c in range(MLA_HEADS // per_blk):
        v = q_rope[:, LANES * c:LANES * (c + 1)]
        vr = v * cr + _roll(v, MLA_ROPE // 2) * s1r + _roll(v, LANES - MLA_ROPE // 2) * s2r
        for hh in range(per_blk):
            mlaq_ref[per_blk * c + hh, :, KV_LORA:MLA_CACHE_W] = (
                vr[:, MLA_ROPE * hh:MLA_ROPE * (hh + 1)].astype(BF16))
    for h in range(MLA_HEADS):
        mlaq_ref[h, :, 0:KV_LORA] = q_lat[:, KV_LORA * h:KV_LORA * (h + 1)].astype(BF16)

    zl = z[:, _AB_LAST:_AB_W]
    lane = lax.broadcasted_iota(I32, zl.shape, 1)
    is_ki = lane < IDX_DIM
    mu = jnp.sum(jnp.where(is_ki, zl, 0.0), -1, keepdims=True) * (1.0 / IDX_DIM)
    dlt = jnp.where(is_ki, zl - mu, 0.0)
    var = jnp.sum(dlt * dlt, -1, keepdims=True) * (1.0 / IDX_DIM)
    y = jnp.where(is_ki, dlt * lax.rsqrt(var + 1e-5) * ikg_ref[...] + ikb_ref[...], zl)
    yr = (y * tab_ref[6]
          + _roll(y, IDX_ROT // 2) * tab_ref[7] + _roll(y, LANES - IDX_ROT // 2) * tab_ref[8]
          + _roll(y, MLA_ROPE // 2) * tab_ref[9] + _roll(y, LANES - MLA_ROPE // 2) * tab_ref[10])
    ik_ref[...] = yr[:, _L_KI:_L_KI + IDX_DIM]
    iw_ref[...] = yr[:, _L_WI:_L_WI + IDX_HEADS] * IDX_W_SCALE
    kva = z[:, _AB_KVA:_AB_KVA + KV_LORA]
    rows_ref[:, 0:KV_LORA] = kva * lax.rsqrt(jnp.mean(kva * kva, -1, keepdims=True) + 1e-6) * kvg_ref[...]
    rows_ref[:, KV_LORA:MLA_CACHE_W] = yr[:, _L_KR:_L_KR + MLA_ROPE]

    for c in range(DSA_HEADS * HEAD_DIM // LANES):
        v = z[:, _AB_QS + LANES * c:_AB_QS + LANES * (c + 1)]
        dq_ref[:, LANES * c:LANES * (c + 1)] = (rot_h(v) * ATTN_SCALE).astype(BF16)
        v = z[:, _AB_QI + LANES * c:_AB_QI + LANES * (c + 1)]
        iq_ref[:, LANES * c:LANES * (c + 1)] = rot_h(v).astype(BF16)
    dk_ref[...] = rot_h(z[:, _AB_KS:_AB_KS + LANES])
    dv_ref[...] = z[:, _AB_VS:_AB_VS + LANES]


def _proj_ab(x, tabs, wts, tm):
    m = x.shape[0]
    w, wqn, wqr, bdk, qg, kvg, ikg, ikb = wts
    row = lambda i: (i, 0)
    full2 = lambda i: (0, 0)
    out_shape = (
        jax.ShapeDtypeStruct((MLA_HEADS, m, MLA_CACHE_W), BF16),
        jax.ShapeDtypeStruct((m, MLA_CACHE_W), F32),
        jax.ShapeDtypeStruct((m, DSA_HEADS * HEAD_DIM), BF16),
        jax.ShapeDtypeStruct((m, DSA_KV_HEADS * HEAD_DIM), F32),
        jax.ShapeDtypeStruct((m, DSA_KV_HEADS * HEAD_DIM), F32),
        jax.ShapeDtypeStruct((m, IDX_HEADS * IDX_DIM), BF16),
        jax.ShapeDtypeStruct((m, IDX_DIM), F32),
        jax.ShapeDtypeStruct((m, IDX_HEADS), F32),
    )
    return pl.pallas_call(
        _proj_ab_kernel,
        grid=(m // tm,),
        in_specs=[
            pl.BlockSpec((tm, D_MODEL), row),
            pl.BlockSpec((tabs.shape[0], tm, LANES), lambda i: (0, i, 0)),
            pl.BlockSpec(w.shape, full2), pl.BlockSpec(wqn.shape, full2), pl.BlockSpec(wqr.shape, full2),
            pl.BlockSpec(bdk.shape, full2), pl.BlockSpec(qg.shape, full2), pl.BlockSpec(kvg.shape, full2),
            pl.BlockSpec(ikg.shape, full2), pl.BlockSpec(ikb.shape, full2),
        ],
        out_specs=(
            pl.BlockSpec((MLA_HEADS, tm, MLA_CACHE_W), lambda i: (0, i, 0)),
            pl.BlockSpec((tm, MLA_CACHE_W), row),
            pl.BlockSpec((tm, DSA_HEADS * HEAD_DIM), row),
            pl.BlockSpec((tm, DSA_KV_HEADS * HEAD_DIM), row),
            pl.BlockSpec((tm, DSA_KV_HEADS * HEAD_DIM), row),
            pl.BlockSpec((tm, IDX_HEADS * IDX_DIM), row),
            pl.BlockSpec((tm, IDX_DIM), row),
            pl.BlockSpec((tm, IDX_HEADS), row),
        ),
        out_shape=out_shape,
        compiler_params=_cparams(("parallel",)),
        name="proj_ab",
    )(x, tabs, w, wqn, wqr, bdk, qg, kvg, ikg, ikb)


def _ab_weights(w_in, q_g, w_q_b, kv_g, w_kv_b, ik_g, ik_b):
    offs = np.cumsum((0,) + AB_SPLITS)
    seg = [w_in[:, offs[k]:offs[k + 1]] for k in range(len(AB_SPLITS))]
    q_a, kv_a, k_r, q_s, k_s, v_s, q_i, k_i, w_i = seg
    pad = jnp.zeros((D_MODEL, LANES - IDX_DIM - MLA_ROPE - IDX_HEADS), w_in.dtype)
    w = jnp.concatenate([q_a, kv_a, q_s, k_s, v_s, q_i, k_i, k_r, w_i, pad], 1).astype(BF16)
    wqn = w_q_b[:, :, :MLA_NOPE].reshape(Q_LORA, MLA_HEADS * MLA_NOPE).astype(BF16)
    wqr = w_q_b[:, :, MLA_NOPE:].reshape(Q_LORA, MLA_HEADS * MLA_ROPE).astype(BF16)
    eye = jnp.eye(MLA_HEADS, dtype=F32)
    bdk = jnp.einsum('chn,hg->hngc', w_kv_b[:, :, :MLA_NOPE], eye).reshape(
        MLA_HEADS * MLA_NOPE, MLA_HEADS * KV_LORA).astype(BF16)
    bdv = jnp.einsum('chv,hg->hcgv', w_kv_b[:, :, MLA_NOPE:], eye).reshape(
        MLA_HEADS * KV_LORA, MLA_HEADS * MLA_V).astype(BF16)
    lpad = jnp.zeros((LANES - IDX_DIM,), F32)
    ikg = jnp.concatenate([ik_g, lpad])[None, :]
    ikb = jnp.concatenate([ik_b, lpad])[None, :]
    return (w, wqn, wqr, bdk, q_g[None, :], kv_g[None, :], ikg, ikb), bdv


def _softmax_step(s, v_bf16, m_s, l_s, acc_s, rows=slice(None)):
    m_prev = m_s[rows, :]
    m_new = jnp.maximum(m_prev, jnp.max(s, -1, keepdims=True))
    alpha = jnp.exp(m_prev - m_new)
    p = jnp.exp(s - m_new)
    l_s[rows, :] = alpha * l_s[rows, :] + jnp.sum(p, -1, keepdims=True)
    acc_s[rows, :] = alpha * acc_s[rows, :] + _dot(p.astype(BF16), v_bf16)
    m_s[rows, :] = m_new


def _softmax_init(m_s, l_s, acc_s):
    m_s[...] = jnp.full(m_s.shape, NEG, F32)
    l_s[...] = jnp.zeros(l_s.shape, F32)
    acc_s[...] = jnp.zeros(acc_s.shape, F32)


def _mla_prompt_kernel(q_ref, rows_ref, bdv_ref, o_ref, m_s, l_s, acc_s, *, tq, tk, n_groups):
    i = pl.program_id(1)
    q = q_ref[...].reshape(MLA_HEADS * tq, MLA_CACHE_W)
    _softmax_init(m_s, l_s, acc_s)

    def step(j, masked):
        kv = rows_ref[pl.ds(pl.multiple_of(j * tk, tk), tk), :].astype(BF16)
        hg = MLA_HEADS // n_groups
        for g in range(n_groups):
            rows = slice(g * hg * tq, (g + 1) * hg * tq)
            s = _dot_t(q[rows], kv) * MLA_SCALE
            if masked:
                kpos = j * tk + lax.broadcasted_iota(I32, (1, tk), 1)
                qpos = i * tq + lax.broadcasted_iota(I32, (hg, tq, 1), 1).reshape(hg * tq, 1)
                s = jnp.where(kpos <= qpos, s, NEG)
            _softmax_step(s, kv[:, 0:KV_LORA], m_s, l_s, acc_s, rows)

    n_full = (i * tq) // tk

    def body(j, carry):
        step(j, False)
        return carry

    lax.fori_loop(0, n_full, body, 0)
    step(n_full, True)
    o = acc_s[...] / l_s[...]
    o_tok = jnp.concatenate([o[h * tq:(h + 1) * tq] for h in range(MLA_HEADS)], axis=1).astype(BF16)
    o_ref[...] = _dot(o_tok, bdv_ref[...]).astype(BF16)


def _mla_prompt(mla_q, rows, bdv, n_b, t, tq, tk, n_groups=4):
    nq = t // tq
    rows8 = MLA_HEADS * tq
    return pl.pallas_call(
        functools.partial(_mla_prompt_kernel, tq=tq, tk=tk, n_groups=n_groups),
        grid=(n_b, nq),
        in_specs=[
            pl.BlockSpec((MLA_HEADS, tq, MLA_CACHE_W), lambda b, i: (0, b * nq + i, 0)),
            pl.BlockSpec((t, MLA_CACHE_W), lambda b, i: (b, 0)),
            pl.BlockSpec(bdv.shape, lambda b, i: (0, 0)),
        ],
        out_specs=pl.BlockSpec((tq, MLA_HEADS * MLA_V), lambda b, i: (b * nq + i, 0)),
        out_shape=jax.ShapeDtypeStruct((n_b * t, MLA_HEADS * MLA_V), BF16),
        scratch_shapes=[pltpu.VMEM((rows8, 1), F32), pltpu.VMEM((rows8, 1), F32),
                        pltpu.VMEM((rows8, KV_LORA), F32)],
        compiler_params=_cparams(("parallel", "parallel")),
        name="mla_prompt",
    )(mla_q, rows, bdv)


def _score_keys(score):
    bits = lax.bitcast_convert_type(score, I32)
    bits = jnp.where(bits == INT_MIN, 0, bits)
    return bits ^ ((bits >> 31) & 0x7FFFFFFF)


def _expand_dsa_q(dq, rows):
    lane = lax.broadcasted_iota(I32, (rows, LANES), 1)
    out = []
    for h in range(DSA_HEADS):
        blk = dq[:, LANES * (h // 2):LANES * (h // 2 + 1)].astype(F32)
        grp = h // DSA_GROUP
        src = blk if (h % 2) == grp else _roll(blk, HEAD_DIM)
        keep = (lane < HEAD_DIM) if grp == 0 else (lane >= HEAD_DIM)
        out.append(jnp.where(keep, src, 0.0).astype(BF16))
    return out


def _stack_idx_q(iq):
    out = []
    for h in range(IDX_HEADS):
        blk = iq[:, LANES * (h // 2):LANES * (h // 2 + 1)].astype(F32)
        src = blk if h % 2 == 0 else _roll(blk, HEAD_DIM)
        out.append(src[:, 0:IDX_DIM].astype(BF16))
    return out


def _collapse_dsa_o(o_heads, rows):
    lane = lax.broadcasted_iota(I32, (rows, LANES), 1)
    cols = []
    for j in range(DSA_HEADS // 2):
        grp = (2 * j) // DSA_GROUP
        a, b = o_heads[2 * j], o_heads[2 * j + 1]
        left = a if grp == 0 else _roll(a, HEAD_DIM)
        right = _roll(b, HEAD_DIM) if grp == 0 else b
        cols.append(jnp.where(lane < HEAD_DIM, left, right))
    return jnp.concatenate(cols, axis=1)


def _tie_bias(k, thr, need, run, tri_ref):
    eqf = jnp.where(k == thr, 1.0, 0.0)
    parts = []
    for blk in range(k.shape[1] // MXU_DIM):
        e = eqf[:, MXU_DIM * blk:MXU_DIM * (blk + 1)]
        parts.append(_dot(e.astype(BF16), tri_ref[...]) + run)
        run = run + jnp.sum(e, -1, keepdims=True)
    pre = parts[0] if len(parts) == 1 else jnp.concatenate(parts, axis=1)
    t_eq = jnp.where(pre < need, 0.0, NEG)
    t_eq = jnp.where(k == INT_MIN, NEG, t_eq)
    bias = jnp.where(k > thr, 0.0, jnp.where(k == thr, t_eq, NEG))
    return bias, run


def _dsa_prompt_kernel(iq_ref, iw_ref, dq_ref, ik_ref, dk_ref, dv_ref, tri_ref, o_ref,
                       key_s, m_s, l_s, acc_s, *, tq, ck, n_sel, n_groups):
    i = pl.program_id(1)
    rows8 = DSA_HEADS * tq
    n_ck = (i * tq + tq + ck - 1) // ck
    qi = jnp.concatenate(_stack_idx_q(iq_ref[...]), axis=0)
    w = iw_ref[...]
    wcol = jnp.concatenate([w[:, h:h + 1] for h in range(IDX_HEADS)], axis=0)
    qpos = i * tq + lax.broadcasted_iota(I32, (tq, 1), 0)

    def score_chunk(c, carry):
        kc = ik_ref[pl.ds(pl.multiple_of(c * ck, ck), ck), :].astype(BF16)
        s = jnp.maximum(_dot_t(qi, kc), 0.0) * wcol
        score = jnp.sum(s.reshape(IDX_HEADS, tq, ck), axis=0)
        kpos = c * ck + lax.broadcasted_iota(I32, (1, ck), 1)
        key_s[c] = jnp.where(kpos <= qpos, _score_keys(score), INT_MIN)
        return carry

    lax.fori_loop(0, n_ck, score_chunk, 0)

    def count(cand, strict):
        def body(c, acc):
            k = key_s[c]
            hit = jnp.where((k > cand) if strict else (k >= cand), 1.0, 0.0)
            for blk in range(ck // LANES):
                acc = acc + hit[:, LANES * blk:LANES * (blk + 1)]
            return acc
        acc = lax.fori_loop(0, n_ck, body, jnp.zeros((tq, LANES), F32))
        return jnp.sum(acc, -1, keepdims=True)

    def bit_step(b, res):
        cand = res | jnp.left_shift(jnp.int32(1), 31 - b)
        return jnp.where(count(cand ^ INT_MIN, False) >= n_sel, cand, res)

    thr = lax.fori_loop(0, 32, bit_step, jnp.zeros((tq, 1), I32)) ^ INT_MIN
    need = n_sel - count(thr, True)

    q = jnp.concatenate(_expand_dsa_q(dq_ref[...], tq), axis=0)
    _softmax_init(m_s, l_s, acc_s)

    def attend_chunk(c, run):
        bias, run = _tie_bias(key_s[c], thr, need, run, tri_ref)
        start = pl.multiple_of(c * ck, ck)
        kb = dk_ref[pl.ds(start, ck), :].astype(BF16)
        vb = dv_ref[pl.ds(start, ck), :].astype(BF16)
        hg = DSA_HEADS // n_groups
        for g in range(n_groups):
            rows = slice(g * hg * tq, (g + 1) * hg * tq)
            s = (_dot_t(q[rows], kb).reshape(hg, tq, ck) + bias[None]).reshape(hg * tq, ck)
            _softmax_step(s, vb, m_s, l_s, acc_s, rows)
        return run

    lax.fori_loop(0, n_ck, attend_chunk, jnp.zeros((tq, 1), F32))
    o = acc_s[...] / l_s[...]
    o_ref[...] = _collapse_dsa_o([o[h * tq:(h + 1) * tq] for h in range(DSA_HEADS)], tq).astype(BF16)


def _dsa_prompt(iq, iw, dq, ik, dk, dv, tri, n_b, t, tq, ck, n_groups=4):
    nq = t // tq
    n_sel = min(TOPK_MAX, t // 4)
    rows8 = DSA_HEADS * tq
    qrow = lambda b, i: (b * nq + i, 0)
    brow = lambda b, i: (b, 0)
    return pl.pallas_call(
        functools.partial(_dsa_prompt_kernel, tq=tq, ck=ck, n_sel=n_sel, n_groups=n_groups),
        grid=(n_b, nq),
        in_specs=[
            pl.BlockSpec((tq, IDX_HEADS * IDX_DIM), qrow),
            pl.BlockSpec((tq, IDX_HEADS), qrow),
            pl.BlockSpec((tq, DSA_HEADS * HEAD_DIM), qrow),
            pl.BlockSpec((t, IDX_DIM), brow),
            pl.BlockSpec((t, DSA_KV_HEADS * HEAD_DIM), brow),
            pl.BlockSpec((t, DSA_KV_HEADS * HEAD_DIM), brow),
            pl.BlockSpec(tri.shape, lambda b, i: (0, 0)),
        ],
        out_specs=pl.BlockSpec((tq, DSA_HEADS * HEAD_DIM), qrow),
        out_shape=jax.ShapeDtypeStruct((n_b * t, DSA_HEADS * HEAD_DIM), BF16),
        scratch_shapes=[pltpu.VMEM((t // ck, tq, ck), I32),
                        pltpu.VMEM((rows8, 1), F32), pltpu.VMEM((rows8, 1), F32),
                        pltpu.VMEM((rows8, LANES), F32)],
        compiler_params=_cparams(("parallel", "parallel")),
        name="dsa_prompt",
    )(iq, iw, dq, ik, dk, dv, tri)


def _page_copies(pool_ref, layer, pt_ref, b, buf_ref, sem_ref, slot, n_pages):
    return [pltpu.make_async_copy(pool_ref.at[layer, pt_ref[b, p]],
                                  buf_ref.at[slot, :, pl.ds(p * PAGE_SIZE, PAGE_SIZE)], sem_ref.at[slot])
            for p in range(n_pages)]


def _fetch_pages(copies, b, n_b, slot):
    @pl.when(b == 0)
    def _():
        for cp in copies(b, slot):
            cp.start()

    @pl.when(b + 1 < n_b)
    def _():
        for cp in copies(b + 1, 1 - slot):
            cp.start()

    for cp in copies(b, slot):
        cp.wait()


def _mla_sample_kernel(pt_ref, q_ref, new_ref, pool_ref, o_ref, buf, sem, *, layer, n_pages):
    b = pl.program_id(0)
    slot = b % 2
    _fetch_pages(lambda bb, sl: _page_copies(pool_ref, layer, pt_ref, bb, buf, sem, sl, n_pages),
                 b, pl.num_programs(0), slot)

    q = q_ref[0]
    kt = buf[slot].astype(BF16)
    s = _dot(q, kt) * MLA_SCALE
    row = new_ref[0].astype(BF16).astype(F32)
    s_new = jnp.sum(q.astype(F32) * row, -1, keepdims=True) * MLA_SCALE
    m = jnp.maximum(jnp.max(s, -1, keepdims=True), s_new)
    p = jnp.exp(s - m)
    p_new = jnp.exp(s_new - m)
    l = jnp.sum(p, -1, keepdims=True) + p_new
    acc = _dot_t(p.astype(BF16), kt[0:KV_LORA]) + p_new.astype(BF16).astype(F32) * row[:, 0:KV_LORA]
    o_ref[0] = acc / l


def _mla_sample(page_table, q, new_rows, pool, layer):
    n_b, n_pages = page_table.shape
    return pl.pallas_call(
        functools.partial(_mla_sample_kernel, layer=layer, n_pages=n_pages),
        grid_spec=pltpu.PrefetchScalarGridSpec(
            num_scalar_prefetch=1,
            grid=(n_b,),
            in_specs=[
                pl.BlockSpec((1, MLA_HEADS, MLA_CACHE_W), lambda b, pt: (b, 0, 0)),
                pl.BlockSpec((1, 1, MLA_CACHE_W), lambda b, pt: (b, 0, 0)),
                pl.BlockSpec(memory_space=pl.ANY),
            ],
            out_specs=pl.BlockSpec((1, MLA_HEADS, KV_LORA), lambda b, pt: (b, 0, 0)),
            scratch_shapes=[pltpu.VMEM((2, MLA_CACHE_W, n_pages * PAGE_SIZE), F32),
                            pltpu.SemaphoreType.DMA((2,))],
        ),
        out_shape=jax.ShapeDtypeStruct((n_b, MLA_HEADS, KV_LORA), F32),
        compiler_params=_cparams(("arbitrary",)),
        name="mla_sample",
    )(page_table, q, new_rows, pool)


def _dsa_sample_kernel(pt_ref, iq_ref, iw_ref, dq_ref, ikn_ref, dkn_ref, dvn_ref, tri_ref,
                       pool_ik, pool_k, pool_v, o_ref,
                       buf_ik, buf_k, buf_v, sem, key_s, bias_s,
                       *, layer, n_pages, n_sel):
    b = pl.program_id(0)
    slot = b % 2

    def copies(bb, sl):
        return (_page_copies(pool_ik, layer, pt_ref, bb, buf_ik, sem.at[0], sl, n_pages)
                + _page_copies(pool_k, layer, pt_ref, bb, buf_k, sem.at[1], sl, n_pages)
                + _page_copies(pool_v, layer, pt_ref, bb, buf_v, sem.at[2], sl, n_pages))

    _fetch_pages(copies, b, pl.num_programs(0), slot)

    n_h = IDX_HEADS
    n_keys = n_pages * PAGE_SIZE
    qi = iq_ref[0]
    wcol = iw_ref[0]

    def all_rows(col):
        wide = jnp.sum(jnp.broadcast_to(col, (n_h, LANES)), axis=0, keepdims=True)
        return jnp.broadcast_to(wide, (n_h, LANES))[:, 0:1]

    row_w = n_keys // n_h
    s = jnp.maximum(_dot(qi, buf_ik[slot].astype(BF16)), 0.0) * wcol
    tot = jnp.sum(s, axis=0, keepdims=True)
    key_s[...] = _score_keys(jnp.concatenate([tot[:, row_w * r:row_w * (r + 1)] for r in range(n_h)], axis=0))
    kn = ikn_ref[0].astype(BF16).astype(F32)
    s_n = jnp.maximum(jnp.sum(qi.astype(F32) * kn, -1, keepdims=True), 0.0) * wcol
    key_new = _score_keys(jnp.broadcast_to(all_rows(s_n), (n_h, LANES)))[:, 0:1]

    def count(cand, strict):
        k = key_s[...]
        hit = jnp.where((k > cand) if strict else (k >= cand), 1.0, 0.0)
        hit_n = jnp.where((key_new > cand) if strict else (key_new >= cand), 1.0, 0.0)
        return all_rows(jnp.sum(hit, -1, keepdims=True)) + hit_n

    def bit_step(bit, res):
        cand = res | jnp.left_shift(jnp.int32(1), 31 - bit)
        return jnp.where(count(cand ^ INT_MIN, False) >= n_sel, cand, res)

    thr = lax.fori_loop(0, 32, bit_step, jnp.zeros((n_h, 1), I32)) ^ INT_MIN
    need = n_sel - count(thr, True)
    n_tie = count(thr, False) - (n_sel - need)
    new_tie = jnp.where(key_new == thr, 1.0, 0.0)

    def spread(bias_d):
        bias_s[...] = jnp.concatenate(
            [jnp.broadcast_to(bias_d[r:r + 1, :], (n_h, row_w)) for r in range(n_h)], axis=1)

    def tie_break():
        k = key_s[...]
        row_tie = jnp.sum(jnp.where(k == thr, 1.0, 0.0), -1, keepdims=True)
        row_id = lax.broadcasted_iota(I32, (n_h, 1), 0)
        before = jnp.zeros((n_h, 1), F32)
        for r in range(n_h - 1):
            before = before + jnp.where(row_id > r, row_tie[r:r + 1, :], 0.0)
        bias_d, _ = _tie_bias(k, thr, need, before, tri_ref)
        spread(bias_d)
        return all_rows(row_tie)

    def admit_all():
        spread(jnp.where(key_s[...] >= thr, 0.0, NEG))
        return n_tie - new_tie

    run = lax.cond(jnp.max(n_tie - need) > 0.0, tie_break, admit_all)
    bias_new = jnp.where(key_new > thr, 0.0, jnp.where(key_new == thr, jnp.where(run < need, 0.0, NEG), NEG))

    q = dq_ref[0]
    s = _dot(q, buf_k[slot].astype(BF16)) + bias_s[...]
    k_n = dkn_ref[0].astype(BF16).astype(F32)
    v_n = dvn_ref[0].astype(BF16).astype(F32)
    s_new = jnp.sum(q.astype(F32) * k_n, -1, keepdims=True) + bias_new
    m = jnp.maximum(jnp.max(s, -1, keepdims=True), s_new)
    p = jnp.exp(s - m)
    p_new = jnp.exp(s_new - m)
    l = jnp.sum(p, -1, keepdims=True) + p_new
    acc = _dot_t(p.astype(BF16), buf_v[slot].astype(BF16)) + p_new.astype(BF16).astype(F32) * v_n
    o = acc / l
    head = lax.broadcasted_iota(I32, (n_h, LANES), 0)
    o = jnp.where(head < DSA_GROUP, o, _roll(o, HEAD_DIM))
    o_ref[0] = o[:, 0:HEAD_DIM]


def _dsa_sample(page_table, iq, iw, dq, ik_new, dk_new, dv_new, tri, pool_ik, pool_k, pool_v, layer):
    n_b, n_pages = page_table.shape
    n_keys = n_pages * PAGE_SIZE
    n_sel = min(TOPK_MAX, (n_keys + 1) // 4)
    per_b = lambda b, pt: (b, 0, 0)
    any_spec = pl.BlockSpec(memory_space=pl.ANY)
    kvw = DSA_KV_HEADS * HEAD_DIM
    return pl.pallas_call(
        functools.partial(_dsa_sample_kernel, layer=layer, n_pages=n_pages, n_sel=n_sel),
        grid_spec=pltpu.PrefetchScalarGridSpec(
            num_scalar_prefetch=1,
            grid=(n_b,),
            in_specs=[
                pl.BlockSpec((1, IDX_HEADS, IDX_DIM), per_b),
                pl.BlockSpec((1, IDX_HEADS, 1), per_b),
                pl.BlockSpec((1, DSA_HEADS, LANES), per_b),
                pl.BlockSpec((1, 1, IDX_DIM), per_b),
                pl.BlockSpec((1, 1, kvw), per_b),
                pl.BlockSpec((1, 1, kvw), per_b),
                pl.BlockSpec(tri.shape, lambda b, pt: (0, 0)),
                any_spec, any_spec, any_spec,
            ],
            out_specs=pl.BlockSpec((1, DSA_HEADS, HEAD_DIM), per_b),
            scratch_shapes=[pltpu.VMEM((2, IDX_DIM, n_keys), F32),
                            pltpu.VMEM((2, kvw, n_keys), F32),
                            pltpu.VMEM((2, kvw, n_keys), F32),
                            pltpu.SemaphoreType.DMA((3, 2)),
                            pltpu.VMEM((IDX_HEADS, n_keys // IDX_HEADS), I32),
                            pltpu.VMEM((IDX_HEADS, n_keys), F32)],
        ),
        out_shape=jax.ShapeDtypeStruct((n_b, DSA_HEADS, HEAD_DIM), F32),
        compiler_params=_cparams(("arbitrary",)),
        name="dsa_sample",
    )(page_table, iq, iw, dq, ik_new, dk_new, dv_new, tri, pool_ik, pool_k, pool_v)


def _proj_c_kernel(x_ref, tab_ref, w_ref, q_ref, k_ref, v_ref):
    z = _dot(x_ref[...].astype(BF16), w_ref[...])
    ch, s1h, s2h = tab_ref[0], tab_ref[1], tab_ref[2]
    width = DIL_HEADS * HEAD_DIM
    for c in range(width // LANES):
        v = z[:, LANES * c:LANES * (c + 1)]
        q_ref[:, LANES * c:LANES * (c + 1)] = (
            v * ch + _roll(v, ROT_DIM // 2) * s1h + _roll(v, LANES - ROT_DIM // 2) * s2h) * ATTN_SCALE
        v = z[:, width + LANES * c:width + LANES * (c + 1)]
        k_ref[:, LANES * c:LANES * (c + 1)] = (
            v * ch + _roll(v, ROT_DIM // 2) * s1h + _roll(v, LANES - ROT_DIM // 2) * s2h)
    v_ref[...] = z[:, 2 * width:3 * width]


def _proj_c(x, tabs, w, tm):
    m = x.shape[0]
    width = DIL_HEADS * HEAD_DIM
    row = lambda i: (i, 0)
    sds = jax.ShapeDtypeStruct((m, width), F32)
    return pl.pallas_call(
        _proj_c_kernel,
        grid=(m // tm,),
        in_specs=[pl.BlockSpec((tm, D_MODEL), row),
                  pl.BlockSpec((3, tm, LANES), lambda i: (0, i, 0)),
                  pl.BlockSpec(w.shape, lambda i: (0, 0))],
        out_specs=(pl.BlockSpec((tm, width), row),) * 3,
        out_shape=(sds, sds, sds),
        compiler_params=_cparams(("parallel",)),
        name="proj_c",
    )(x, tabs, w)


def _dil_prompt_kernel(q_ref, k_ref, v_ref, o_ref, og_s, lse_s, *, t, tu, mc):
    lane = lax.broadcasted_iota(I32, (tu, LANES), 1)
    low = lane < HEAD_DIM
    row2 = lax.broadcasted_iota(I32, (2, tu, 1), 1).reshape(2 * tu, 1)

    for g, (window, dil) in enumerate(DIL_PATTERNS):
        n_ub = t // dil // tu
        back = window // dil

        def unit(u, carry, g=g, dil=dil, n_ub=n_ub, back=back):
            r = u // n_ub
            ub = u % n_ub
            q_start = r + dil * tu * ub
            qf = q_ref[pl.ds(q_start, tu, stride=dil), :]
            q2 = jnp.concatenate([jnp.where(low, qf, 0.0), jnp.where(low, 0.0, qf)], axis=0).astype(BF16)
            u0 = jnp.maximum(ub * tu - back, 0)
            k_start = r + dil * u0
            kb = k_ref[pl.ds(k_start, 2 * tu, stride=dil), :].astype(BF16)
            vb = v_ref[pl.ds(k_start, 2 * tu, stride=dil), :].astype(BF16)
            s = _dot_t(q2, kb)
            uq = ub * tu + row2
            uk = u0 + lax.broadcasted_iota(I32, (1, 2 * tu), 1)
            s = jnp.where((uk <= uq) & (uk >= uq - back), s, NEG)
            m = jnp.max(s, -1, keepdims=True)
            p = jnp.exp(s - m)
            l = jnp.sum(p, -1, keepdims=True)
            o = _dot(p.astype(BF16), vb) / l
            lse = m + jnp.log(l)
            og_s[g, pl.ds(q_start, tu, stride=dil), :] = jnp.where(low, o[0:tu], o[tu:2 * tu])
            lse_s[g, pl.ds(q_start, tu, stride=dil), :] = jnp.where(
                low, jnp.broadcast_to(lse[0:tu], (tu, LANES)), jnp.broadcast_to(lse[tu:2 * tu], (tu, LANES)))
            return carry

        lax.fori_loop(0, dil * n_ub, unit, 0, unroll=4)

    def merge(c, carry):
        sl = pl.ds(pl.multiple_of(c * mc, mc), mc)
        l0, l1, l2 = lse_s[0, sl, :], lse_s[1, sl, :], lse_s[2, sl, :]
        mx = jnp.maximum(jnp.maximum(l0, l1), l2)
        w0, w1, w2 = jnp.exp(l0 - mx), jnp.exp(l1 - mx), jnp.exp(l2 - mx)
        o = (w0 * og_s[0, sl, :] + w1 * og_s[1, sl, :] + w2 * og_s[2, sl, :]) / (w0 + w1 + w2)
        o_ref[sl, :] = o.astype(BF16)
        return carry

    lax.fori_loop(0, t // mc, merge, 0)


def _dil_prompt(q, k, v, n_b, t, tu=128, mc=512):
    n_pair = DIL_HEADS * HEAD_DIM // LANES
    spec = pl.BlockSpec((t, LANES), lambda b, p: (b, p))
    return pl.pallas_call(
        functools.partial(_dil_prompt_kernel, t=t, tu=tu, mc=mc),
        grid=(n_b, n_pair),
        in_specs=[spec, spec, spec],
        out_specs=spec,
        out_shape=jax.ShapeDtypeStruct((n_b * t, DIL_HEADS * HEAD_DIM), BF16),
        scratch_shapes=[pltpu.VMEM((len(DIL_PATTERNS), t, LANES), F32),
                        pltpu.VMEM((len(DIL_PATTERNS), t, LANES), F32)],
        compiler_params=_cparams(("parallel", "parallel")),
        name="dil_prompt",
    )(q, k, v)


def _dil_sample_kernel(*refs, fb, has_prev):
    q_ref, kr_ref, vr_ref, kc_ref, vc_ref, sk_ref, sv_ref = refs[:7]
    o_ref, ko_ref, vo_ref = refs[-3:]
    del has_prev
    w = WIN_MAX
    q = q_ref[0]
    rows = 8
    head_of_lane = lax.broadcasted_iota(I32, (rows, fb), 1) // HEAD_DIM
    own = head_of_lane == lax.broadcasted_iota(I32, (rows, fb), 0)
    qe16 = jnp.where(own, jnp.broadcast_to(q, (rows, fb)), 0.0).astype(BF16)
    kt = sk_ref[0, 0]
    vt = sv_ref[0, 0]
    s = _dot(qe16, kt.astype(BF16))
    t = lax.broadcasted_iota(I32, (1, w), 1)
    cnt = jnp.zeros((1, w), F32)
    for window, dil in DIL_PATTERNS:
        cnt = cnt + jnp.where((t >= w - window) & ((t & (dil - 1)) == 0), 1.0, 0.0)
    s = jnp.where(cnt > 0.0, s, NEG)
    k_n = kr_ref[0].astype(BF16).astype(F32)
    v_n = vr_ref[0].astype(BF16).astype(F32)
    s_new = jnp.sum(qe16.astype(F32) * k_n, -1, keepdims=True)
    m = jnp.maximum(jnp.max(s, -1, keepdims=True), s_new)
    p = jnp.exp(s - m) * cnt
    p_new = jnp.exp(s_new - m)
    n_pat = float(len(DIL_PATTERNS))
    l = jnp.sum(p, -1, keepdims=True) + n_pat * p_new
    acc = _dot_t(p.astype(BF16), vt.astype(BF16)) + n_pat * p_new.astype(BF16).astype(F32) * v_n
    o_ref[0] = jnp.sum(jnp.where(own, acc / l, 0.0), axis=0, keepdims=True)
    last = lax.broadcasted_iota(I32, (fb, w), 1) == w - 1
    ko_ref[0, 0] = jnp.where(last, kc_ref[0], _roll(kt, w - 1))
    vo_ref[0, 0] = jnp.where(last, vc_ref[0], _roll(vt, w - 1))


def _dil_sample(q, k_new, v_new, state_k, state_v, layer, prev=None, fb=256):
    n_b, width, w = state_k.shape[1:]
    row = pl.BlockSpec((1, 1, fb), lambda b, f: (b, 0, f))
    col = pl.BlockSpec((1, fb, 1), lambda b, f: (b, f, 0))
    st = pl.BlockSpec((1, 1, fb, w), lambda b, f: (layer, b, f, 0))
    ins = [q[:, None, :], k_new[:, None, :], v_new[:, None, :], k_new[:, :, None], v_new[:, :, None],
           state_k, state_v]
    in_specs = [row, row, row, col, col, st, st]
    aliases = {}
    if prev is not None:
        ins += list(prev)
        in_specs += [pl.BlockSpec(memory_space=pl.ANY)] * 2
        aliases = {7: 1, 8: 2}
    sds = jax.ShapeDtypeStruct(state_k.shape, F32)
    return pl.pallas_call(
        functools.partial(_dil_sample_kernel, fb=fb, has_prev=prev is not None),
        grid=(n_b, width // fb),
        in_specs=in_specs,
        out_specs=(row, st, st),
        out_shape=(jax.ShapeDtypeStruct((n_b, 1, width), F32), sds, sds),
        input_output_aliases=aliases,
        compiler_params=_cparams(("parallel", "parallel")),
        name="dil_sample",
    )(*ins)


def _mm_kernel(a_ref, w_ref, o_ref):
    o_ref[...] = _dot(a_ref[...].astype(BF16), w_ref[...]).astype(o_ref.dtype)


def _mm(a, w, out_dtype, tm):
    m, k = a.shape
    n = w.shape[1]
    return pl.pallas_call(
        _mm_kernel,
        grid=(m // tm,),
        in_specs=[pl.BlockSpec((tm, k), lambda i: (i, 0)), pl.BlockSpec((k, n), lambda i: (0, 0))],
        out_specs=pl.BlockSpec((tm, n), lambda i: (i, 0)),
        out_shape=jax.ShapeDtypeStruct((m, n), out_dtype),
        compiler_params=_cparams(("parallel",)),
        name="mm",
    )(a, w)


def _mm_ln_kernel(*refs, n_in):
    x_ref = refs[0]
    a_refs = refs[1:1 + n_in]
    w_refs = refs[1 + n_in:1 + 2 * n_in]
    g_ref, b_ref, o_ref = refs[1 + 2 * n_in:]
    acc = DEEPNORM_ALPHA * x_ref[...]
    for a_ref, w_ref in zip(a_refs, w_refs):
        acc = acc + _dot(a_ref[...].astype(BF16), w_ref[...])
    mu = jnp.mean(acc, -1, keepdims=True)
    dlt = acc - mu
    var = jnp.mean(dlt * dlt, -1, keepdims=True)
    o_ref[...] = dlt * lax.rsqrt(var + 1e-5) * g_ref[...] + b_ref[...]


def _mm_ln(x, a_list, w_list, g, b, tm):
    m = x.shape[0]
    n_in = len(a_list)
    row = lambda i: (i, 0)
    fix = lambda i: (0, 0)
    in_specs = ([pl.BlockSpec((tm, D_MODEL), row)]
                + [pl.BlockSpec((tm, a.shape[1]), row) for a in a_list]
                + [pl.BlockSpec(w.shape, fix) for w in w_list]
                + [pl.BlockSpec((1, D_MODEL), fix), pl.BlockSpec((1, D_MODEL), fix)])
    return pl.pallas_call(
        functools.partial(_mm_ln_kernel, n_in=n_in),
        grid=(m // tm,),
        in_specs=in_specs,
        out_specs=pl.BlockSpec((tm, D_MODEL), row),
        out_shape=jax.ShapeDtypeStruct((m, D_MODEL), F32),
        compiler_params=_cparams(("parallel",)),
        name="mm_ln",
    )(x, *a_list, *w_list, g[None, :], b[None, :])


def _swiglu_up_kernel(x_ref, wg_ref, wu_ref, o_ref):
    x = x_ref[...].astype(BF16)
    gate = _dot(x, wg_ref[...])
    up = _dot(x, wu_ref[...])
    o_ref[...] = (gate * jax.nn.sigmoid(gate) * up).astype(BF16)


def _swiglu_up(x, wg, wu, tm):
    m = x.shape[0]
    d_ff = wg.shape[1]
    return pl.pallas_call(
        _swiglu_up_kernel,
        grid=(m // tm,),
        in_specs=[pl.BlockSpec((tm, D_MODEL), lambda i: (i, 0)),
                  pl.BlockSpec(wg.shape, lambda i: (0, 0)), pl.BlockSpec(wu.shape, lambda i: (0, 0))],
        out_specs=pl.BlockSpec((tm, d_ff), lambda i: (i, 0)),
        out_shape=jax.ShapeDtypeStruct((m, d_ff), BF16),
        compiler_params=_cparams(("parallel",)),
        name="swiglu_up",
    )(x, wg, wu)


def _row_tile(m):
    for tm in (256, 128):
        if m % tm == 0:
            return tm
    return m


def kernel(x_prompt, x_sample, cache_mla, cache_dsa_k, cache_dsa_v, cache_idx_k, state_win_k, state_win_v,
           page_table, w_in_ab, mla_q_g, w_mla_q_b, mla_kv_g, w_mla_kv_b, idx_k_g, idx_k_b, w_o_ab,
           w_in_c, w_o_c, ln_g, ln_b, w_gate, w_up, w_down):
    n_b, t = x_prompt.shape[:2]
    n_dec, n_new = x_sample.shape[:2]
    n_pages = page_table.shape[1]
    past = n_pages * PAGE_SIZE
    win_buf = state_win_k.shape[2]
    assert n_new == 1 and win_buf == WIN_MAX and t % (16 * 256) == 0 and n_pages % 8 == 0
    mp, ms = n_b * t, n_dec
    tm_p, tm_s = _row_tile(mp), _row_tile(ms)

    pos_p = jnp.tile(jnp.arange(t, dtype=I32), n_b)
    pos_s = jnp.full((ms,), past, I32)
    tab_ab_p, tab_ab_s = _ab_tables(pos_p), _ab_tables(pos_s)
    tab_h_p, tab_h_s = _head_tables(pos_p), _head_tables(pos_s)
    tri = jnp.triu(jnp.ones((MXU_DIM, MXU_DIM), F32), 1).astype(BF16)
    n_l, n_pool = cache_mla.shape[:2]
    pool_kw = DSA_KV_HEADS * HEAD_DIM
    pool_mla = jnp.transpose(cache_mla, (0, 1, 3, 2))
    pool_ik = jnp.transpose(cache_idx_k, (0, 1, 3, 2))
    pool_k = jnp.transpose(cache_dsa_k, (0, 1, 3, 4, 2)).reshape(n_l, n_pool, pool_kw, PAGE_SIZE)
    pool_v = jnp.transpose(cache_dsa_v, (0, 1, 3, 4, 2)).reshape(n_l, n_pool, pool_kw, PAGE_SIZE)
    width_c = DIL_HEADS * HEAD_DIM
    win_k = jnp.transpose(state_win_k, (0, 1, 3, 4, 2)).reshape(state_win_k.shape[0], n_dec, width_c, win_buf)
    win_v = jnp.transpose(state_win_v, (0, 1, 3, 4, 2)).reshape(state_win_v.shape[0], n_dec, width_c, win_buf)
    new_win = None

    hp = x_prompt.reshape(mp, D_MODEL)
    hs = x_sample.reshape(ms, D_MODEL)
    outs = {k: [] for k in ("mla_p", "mla_s", "dk_p", "dk_s", "dv_p", "dv_s", "ik_p", "ik_s",
                            "wk_p", "wk_s", "wv_p", "wv_s")}
    for layer in range(DEPTH):
        if layer % 2 == 0:
            i = layer // 2
            wts, bdv = _ab_weights(w_in_ab[i], mla_q_g[i], w_mla_q_b[i], mla_kv_g[i], w_mla_kv_b[i],
                                   idx_k_g[i], idx_k_b[i])
            w_o = w_o_ab[i].astype(BF16)
            w_o_mla, w_o_dsa = w_o[:MLA_HEADS * MLA_V], w_o[MLA_HEADS * MLA_V:]
            mla_q, rows, dq, dk, dv, iq, ik, iw = _proj_ab(hp, tab_ab_p, wts, tm_p)
            o_mla = _mla_prompt(mla_q, rows, bdv, n_b, t, tq=128, tk=512)
            o_dsa = _dsa_prompt(iq, iw, dq, ik, dk, dv, tri, n_b, t, tq=128, ck=512)
            mix_in_p = ([o_mla, o_dsa], [w_o_mla, w_o_dsa])
            outs["mla_p"].append(rows.reshape(n_b, t, MLA_CACHE_W))
            outs["dk_p"].append(dk.reshape(n_b, t, DSA_KV_HEADS, HEAD_DIM))
            outs["dv_p"].append(dv.reshape(n_b, t, DSA_KV_HEADS, HEAD_DIM))
            outs["ik_p"].append(ik.reshape(n_b, t, IDX_DIM))
            mla_q, rows, dq, dk, dv, iq, ik, iw = _proj_ab(hs, tab_ab_s, wts, tm_s)
            o_lat = _mla_sample(page_table, jnp.transpose(mla_q, (1, 0, 2)), rows[:, None, :], pool_mla, i)
            o_mla = _mm(o_lat.reshape(ms, MLA_HEADS * KV_LORA), bdv, BF16, tm_s)
            iq_h = iq.reshape(ms, IDX_HEADS, IDX_DIM)
            dq_h = dq.reshape(ms, DSA_KV_HEADS, DSA_GROUP, HEAD_DIM)
            zero = jnp.zeros_like(dq_h[:, 0])
            dq_e = jnp.concatenate([jnp.concatenate([dq_h[:, 0], zero], -1),
                                    jnp.concatenate([zero, dq_h[:, 1]], -1)], 1)
            o_e = _dsa_sample(page_table, iq_h, iw[:, :, None], dq_e, ik[:, None, :], dk[:, None, :],
                              dv[:, None, :], tri, pool_ik, pool_k, pool_v, i)
            mix_in_s = ([o_mla, o_e.reshape(ms, DSA_HEADS * HEAD_DIM)], [w_o_mla, w_o_dsa])
            outs["mla_s"].append(rows.reshape(n_dec, n_new, MLA_CACHE_W))
            outs["dk_s"].append(dk.reshape(n_dec, n_new, DSA_KV_HEADS, HEAD_DIM))
            outs["dv_s"].append(dv.reshape(n_dec, n_new, DSA_KV_HEADS, HEAD_DIM))
            outs["ik_s"].append(ik.reshape(n_dec, n_new, IDX_DIM))
        else:
            j = layer // 2
            w_c = w_in_c[j].astype(BF16)
            w_o = w_o_c[j].astype(BF16)
            q, k, v = _proj_c(hp, tab_h_p, w_c, tm_p)
            o = _dil_prompt(q, k, v, n_b, t)
            mix_in_p = ([o], [w_o])
            keep = min(WIN_MAX, t)
            outs["wk_p"].append(k.reshape(n_b, t, DIL_HEADS, HEAD_DIM)[:, t - keep:])
            outs["wv_p"].append(v.reshape(n_b, t, DIL_HEADS, HEAD_DIM)[:, t - keep:])
            q, k, v = _proj_c(hs, tab_h_s, w_c, tm_s)
            o, new_k, new_v = _dil_sample(q, k, v, win_k, win_v, j, prev=new_win)
            new_win = (new_k, new_v)
            mix_in_s = ([o.reshape(ms, DIL_HEADS * HEAD_DIM)], [w_o])
        wg, wu, wd = w_gate[layer].astype(BF16), w_up[layer].astype(BF16), w_down[layer].astype(BF16)
        hp = _mm_ln(hp, *mix_in_p, ln_g[layer, 0], ln_b[layer, 0], tm_p)
        hp = _mm_ln(hp, [_swiglu_up(hp, wg, wu, tm_p)], [wd], ln_g[layer, 1], ln_b[layer, 1], tm_p)
        hs = _mm_ln(hs, *mix_in_s, ln_g[layer, 0], ln_b[layer, 0], tm_s)
        hs = _mm_ln(hs, [_swiglu_up(hs, wg, wu, tm_s)], [wd], ln_g[layer, 1], ln_b[layer, 1], tm_s)
    st = {k: jnp.stack(v) for k, v in outs.items() if v}
    wk_s, wv_s = (jnp.transpose(a.reshape(a.shape[0], n_dec, DIL_HEADS, HEAD_DIM, win_buf), (0, 1, 4, 2, 3))
                  for a in new_win)
    return (hp.reshape(n_b, t, D_MODEL), hs.reshape(n_dec, n_new, D_MODEL),
            st["mla_p"], st["mla_s"], st["dk_p"], st["dk_s"], st["dv_p"], st["dv_s"],
            st["ik_p"], st["ik_s"], st["wk_p"], wk_s, st["wv_p"], wv_s)
```
